```python
import numpy as np
import jax
import jax.numpy as jnp
from jax import lax

D_MODEL = 4096
BATCH = 8
SEQ = 2048
DEPTH = 1

HEAD_DIM = 128
MIX_WIDTH = D_MODEL
POOL_WINDOWS = (2, 4, 8, 16)
POOL_WIDTH = MIX_WIDTH // 4
POOL_GROUP = POOL_WIDTH // len(POOL_WINDOWS)
NSA_WIDTH = MIX_WIDTH - POOL_WIDTH
NSA_HEADS = NSA_WIDTH // HEAD_DIM
NSA_GROUP = 6
NSA_KV_HEADS = NSA_HEADS // NSA_GROUP
N_BRANCH = 3
KV_WIDTH = N_BRANCH * 2 * NSA_KV_HEADS * HEAD_DIM
GATE_WIDTH = NSA_HEADS * N_BRANCH
IN_WIDTH = POOL_WIDTH + NSA_WIDTH + KV_WIDTH + GATE_WIDTH
CMP_BLOCK = 32
CMP_STRIDE = 16
CMP_HIDDEN = 256
SEL_BLOCK = 64
SEL_TOPN = 16
WINDOW = 512
Q_BLOCK = 16
FORCE_BONUS = 1000.0
NEG_SCORE = -1e9
ROPE_THETA = 10000.0
PEER_HEADS = 8
PEER_KEY_DIM = 256
PEER_NKEYS = 128
PEER_EXPERTS = PEER_NKEYS * PEER_NKEYS
PEER_TOPK = 16
PEER_CHUNK = 64
NORM_EPS = 1e-6
N_MOD = 6

kernel_name = 'hybrid_pool_nsa_peer_block'


def rmsnorm(x, g):
    xf = x.astype(jnp.float32)
    y = xf * lax.rsqrt(jnp.mean(xf * xf, axis=-1, keepdims=True) + NORM_EPS)
    return (y * g.astype(jnp.float32)).astype(x.dtype)


def rope(x, pos):
    half = HEAD_DIM // 2
    inv = ROPE_THETA ** (-jnp.arange(half, dtype=jnp.float32) / half)
    ang = pos[:, None] * inv[None, :]
    cos = jnp.cos(ang)[:, None, :]
    sin = jnp.sin(ang)[:, None, :]
    xf = x.astype(jnp.float32)
    x1, x2 = xf[..., :half], xf[..., half:]
    return jnp.concatenate([x1 * cos - x2 * sin, x2 * cos + x1 * sin], axis=-1).astype(x.dtype)


def masked_softmax(s, mask):
    s = jnp.where(mask, s.astype(jnp.float32), -jnp.inf)
    m = jnp.max(s, axis=-1, keepdims=True)
    m = jnp.where(jnp.isfinite(m), m, 0.0)
    p = jnp.exp(s - m)
    den = jnp.sum(p, axis=-1, keepdims=True)
    return p / jnp.where(den > 0, den, 1.0)


def pool_mixer(p, w_pool, pool_scale):
    B_, T_, _ = p.shape
    pg = p.reshape(B_, T_, len(POOL_WINDOWS), POOL_GROUP).astype(jnp.float32)
    cs = jnp.cumsum(pg, axis=1)
    count = jnp.arange(1, T_ + 1, dtype=jnp.float32)[None, :, None]
    groups = []
    for gi, w in enumerate(POOL_WINDOWS):
        c_g = cs[:, :, gi]
        lag = jnp.pad(c_g, ((0, 0), (w, 0), (0, 0)))[:, :T_]
        groups.append((c_g - lag) / jnp.minimum(count, float(w)) - pg[:, :, gi])
    d = jnp.stack(groups, axis=2).astype(p.dtype)
    y = jnp.einsum('btgc,gcd->btgd', d, w_pool)
    return y.reshape(B_, T_, POOL_WIDTH) * pool_scale


def compress_blocks(k_raw, blk, pe, w1, w2):
    kb = k_raw[:, blk] + pe[None, None, :, None, :]
    hid = jax.nn.gelu(jnp.einsum('bnlgd,ldh->bngh', kb, w1), approximate=False)
    return jnp.einsum('bngh,hd->bngd', hid, w2)


def cmp_to_sel_matrix(n_cmp, n_sel):
    cst = np.arange(n_cmp)[:, None] * CMP_STRIDE
    sst = np.arange(n_sel)[None, :] * SEL_BLOCK
    ov = np.clip(np.minimum(cst + CMP_BLOCK, sst + SEL_BLOCK) - np.maximum(cst, sst), 0, None)
    return jnp.asarray(ov / CMP_BLOCK, dtype=jnp.float32)


def nsa_mixer(q_in, kv_in, g_in, q_norm_g, k_norm_g, cmp_pe, cmp_w1, cmp_w2):
    B_, T_, _ = q_in.shape
    G, R, Dh = NSA_KV_HEADS, NSA_GROUP, HEAD_DIM
    pos = jnp.arange(T_, dtype=jnp.float32)
    qn = rmsnorm(q_in.reshape(B_, T_, NSA_HEADS, Dh), q_norm_g)
    qr = rope(qn, pos).reshape(B_, T_, G, R, Dh)
    qn = qn.reshape(B_, T_, G, R, Dh)
    kv = kv_in.reshape(B_, T_, N_BRANCH, 2, G, Dh)
    n_cmp = (T_ - CMP_BLOCK) // CMP_STRIDE + 1
    blk = np.arange(n_cmp)[:, None] * CMP_STRIDE + np.arange(CMP_BLOCK)[None, :]
    k_cmp = rmsnorm(compress_blocks(kv[:, :, 0, 0], blk, cmp_pe[0], cmp_w1[0], cmp_w2[0]), k_norm_g[0])
    v_cmp = compress_blocks(kv[:, :, 0, 1], blk, cmp_pe[1], cmp_w1[1], cmp_w2[1])
    cmp_end = jnp.arange(n_cmp) * CMP_STRIDE + (CMP_BLOCK - 1)
    n_sel = T_ // SEL_BLOCK
    top_n = min(SEL_TOPN, n_sel)
    sel_map = cmp_to_sel_matrix(n_cmp, n_sel)
    sel_ids = jnp.arange(n_sel)
    sel_start = sel_ids * SEL_BLOCK
    k_sel = rope(rmsnorm(kv[:, :, 1, 0], k_norm_g[1]), pos)
    k_sel = k_sel.reshape(B_, n_sel, SEL_BLOCK, G, Dh).transpose(0, 3, 1, 2, 4)
    v_sel = kv[:, :, 1, 1].reshape(B_, n_sel, SEL_BLOCK, G, Dh).transpose(0, 3, 1, 2, 4)
    pad = ((0, 0), (WINDOW, 0), (0, 0), (0, 0))
    k_win = jnp.pad(rope(rmsnorm(kv[:, :, 2, 0], k_norm_g[2]), pos), pad)
    v_win = jnp.pad(kv[:, :, 2, 1], pad)
    gates = jax.nn.sigmoid(g_in.astype(jnp.float32)).reshape(B_, T_, G, R, N_BRANCH)
    b_ix = jnp.arange(B_)[:, None, None, None]
    g_ix = jnp.arange(G)[None, None, :, None]
    scale = Dh ** -0.5

    def block_step(qb):
        t0 = qb * Q_BLOCK
        t = t0 + jnp.arange(Q_BLOCK)
        qn_b = lax.dynamic_slice_in_dim(qn, t0, Q_BLOCK, axis=1)
        qr_b = lax.dynamic_slice_in_dim(qr, t0, Q_BLOCK, axis=1)
        g_b = lax.dynamic_slice_in_dim(gates, t0, Q_BLOCK, axis=1)
        s_c = jnp.einsum('bqgrd,bngd->bqgrn', qn_b, k_cmp) * scale
        m_c = (cmp_end[None, :] <= t[:, None])[None, :, None, None, :]
        p_c = masked_softmax(s_c, m_c)
        o_c = jnp.einsum('bqgrn,bngd->bqgrd', p_c, v_cmp)
        imp = jnp.einsum('bqgrn,nj->bqgj', p_c, sel_map)
        cur = t // SEL_BLOCK
        forced = (sel_ids[None, :] == 0) | (sel_ids[None, :] == cur[:, None]) | (sel_ids[None, :] == cur[:, None] - 1)
        valid = sel_start[None, :] <= t[:, None]
        score = jnp.where(valid[None, :, None, :], imp + jnp.where(forced, FORCE_BONUS, 0.0)[None, :, None, :], NEG_SCORE)
        _, idx = lax.top_k(score, top_n)
        k_g = k_sel[b_ix, g_ix, idx]
        v_g = v_sel[b_ix, g_ix, idx]
        s_s = jnp.einsum('bqgrd,bqgkld->bqgrkl', qr_b, k_g) * scale
        tok = idx[..., None] * SEL_BLOCK + jnp.arange(SEL_BLOCK)
        m_s = (tok <= t[None, :, None, None, None]).reshape(B_, Q_BLOCK, G, 1, top_n * SEL_BLOCK)
        p_s = masked_softmax(s_s.reshape(B_, Q_BLOCK, G, R, top_n * SEL_BLOCK), m_s)
        o_s = jnp.einsum('bqgrn,bqgnd->bqgrd', p_s, v_g.reshape(B_, Q_BLOCK, G, top_n * SEL_BLOCK, Dh))
        k_wb = lax.dynamic_slice_in_dim(k_win, t0, WINDOW + Q_BLOCK, axis=1)
        v_wb = lax.dynamic_slice_in_dim(v_win, t0, WINDOW + Q_BLOCK, axis=1)
        s_w = jnp.einsum('bqgrd,bsgd->bqgrs', qr_b, k_wb) * scale
        spos = t0 - WINDOW + jnp.arange(WINDOW + Q_BLOCK)
        dist = t[:, None] - spos[None, :]
        m_w = ((dist >= 0) & (dist < WINDOW) & (spos[None, :] >= 0))[None, :, None, None, :]
        p_w = masked_softmax(s_w, m_w)
        o_w = jnp.einsum('bqgrs,bsgd->bqgrd', p_w, v_wb)
        o = g_b[..., 0:1] * o_c + g_b[..., 1:2] * o_s + g_b[..., 2:3] * o_w
        return o.reshape(B_, Q_BLOCK, NSA_WIDTH).astype(q_in.dtype)

    out = lax.map(block_step, jnp.arange(T_ // Q_BLOCK))
    return out.transpose(1, 0, 2, 3).reshape(B_, T_, NSA_WIDTH)


def peer_ffn(h, w_pq, peer_keys, peer_u, peer_v):
    B_, T_, D_ = h.shape
    q = jnp.dot(h, w_pq).reshape(B_, T_, PEER_HEADS, 2, PEER_KEY_DIM // 2).astype(jnp.float32)
    s = jnp.einsum('bthcd,hckd->bthck', q, peer_keys.astype(jnp.float32))
    v, i = lax.top_k(s, PEER_TOPK)
    cand_s = (v[..., 0, :, None] + v[..., 1, None, :]).reshape(B_, T_, PEER_HEADS, PEER_TOPK * PEER_TOPK)
    cand_i = (i[..., 0, :, None] * PEER_NKEYS + i[..., 1, None, :]).reshape(B_, T_, PEER_HEADS, PEER_TOPK * PEER_TOPK)
    top_s, sel = lax.top_k(cand_s, PEER_TOPK)
    e_idx = jnp.take_along_axis(cand_i, sel, axis=-1)
    g = jax.nn.softmax(top_s, axis=-1)
    n_chunks = (B_ * T_) // PEER_CHUNK
    xt = h.reshape(n_chunks, PEER_CHUNK, D_)
    et = e_idx.reshape(n_chunks, PEER_CHUNK, PEER_HEADS * PEER_TOPK)
    gt = g.reshape(n_chunks, PEER_CHUNK, PEER_HEADS * PEER_TOPK)

    def chunk(args):
        xc, ec, gc = args
        u = peer_u[ec]
        a = jax.nn.gelu(jnp.einsum('nd,nkd->nk', xc, u).astype(jnp.float32), approximate=False)
        return jnp.einsum('nk,nkd->nd', (gc * a).astype(xc.dtype), peer_v[ec])

    y = lax.map(chunk, (xt, et, gt))
    return y.reshape(B_, T_, D_)


def hybrid_layer(x, c, w_ada, b_ada, norm1_g, norm2_g, w_in, w_out, w_pool, pool_scale,
                 q_norm_g, k_norm_g, cmp_pe, cmp_w1, cmp_w2, w_pq, peer_keys, peer_u, peer_v):
    mod = jnp.dot(jax.nn.silu(c), w_ada) + b_ada
    shift1, scale1, gate1, shift2, scale2, gate2 = [m[:, None, :] for m in jnp.split(mod, N_MOD, axis=-1)]
    h = rmsnorm(x, norm1_g) * (1.0 + scale1) + shift1
    proj = jnp.dot(h, w_in)
    p_in, q_in, kv_in, g_in = jnp.split(
        proj, [POOL_WIDTH, POOL_WIDTH + NSA_WIDTH, POOL_WIDTH + NSA_WIDTH + KV_WIDTH], axis=-1)
    y_pool = pool_mixer(p_in, w_pool, pool_scale)
    y_nsa = nsa_mixer(q_in, kv_in, g_in, q_norm_g, k_norm_g, cmp_pe, cmp_w1, cmp_w2)
    x = x + gate1 * jnp.dot(jnp.concatenate([y_pool, y_nsa], axis=-1), w_out)
    h = rmsnorm(x, norm2_g) * (1.0 + scale2) + shift2
    x = x + gate2 * peer_ffn(h, w_pq, peer_keys, peer_u, peer_v)
    return x


def setup_inputs(seed: int = 0) -> dict:
    key = jax.random.key(seed)
    ks = jax.random.split(key, 19)

    def nrm(k, shape, scale):
        return jax.random.normal(k, shape, jnp.float32) * scale

    return {
        'x': nrm(ks[0], (BATCH, SEQ, D_MODEL), 1.0),
        'c': nrm(ks[1], (BATCH, D_MODEL), 1.0),
        'w_ada': nrm(ks[2], (DEPTH, D_MODEL, N_MOD * D_MODEL), D_MODEL ** -0.5),
        'b_ada': nrm(ks[3], (DEPTH, N_MOD * D_MODEL), 0.01),
        'norm1_g': 1.0 + nrm(ks[4], (DEPTH, D_MODEL), 0.01),
        'norm2_g': 1.0 + nrm(ks[5], (DEPTH, D_MODEL), 0.01),
        'w_in': nrm(ks[6], (DEPTH, D_MODEL, IN_WIDTH), D_MODEL ** -0.5),
        'w_out': nrm(ks[7], (DEPTH, MIX_WIDTH, D_MODEL), MIX_WIDTH ** -0.5),
        'w_pool': nrm(ks[8], (DEPTH, len(POOL_WINDOWS), POOL_GROUP, POOL_GROUP), POOL_GROUP ** -0.5),
        'pool_scale': 1.0 + nrm(ks[9], (DEPTH, POOL_WIDTH), 0.02),
        'q_norm_g': 1.0 + nrm(ks[10], (DEPTH, HEAD_DIM), 0.01),
        'k_norm_g': 1.0 + nrm(ks[11], (DEPTH, N_BRANCH, HEAD_DIM), 0.01),
        'cmp_pe': nrm(ks[12], (DEPTH, 2, CMP_BLOCK, HEAD_DIM), 0.02),
        'cmp_w1': nrm(ks[13], (DEPTH, 2, CMP_BLOCK, HEAD_DIM, CMP_HIDDEN), (CMP_BLOCK * HEAD_DIM) ** -0.5),
        'cmp_w2': nrm(ks[14], (DEPTH, 2, CMP_HIDDEN, HEAD_DIM), CMP_HIDDEN ** -0.5),
        'w_pq': nrm(ks[15], (DEPTH, D_MODEL, PEER_HEADS * PEER_KEY_DIM), D_MODEL ** -0.5),
        'peer_keys': nrm(ks[16], (DEPTH, PEER_HEADS, 2, PEER_NKEYS, PEER_KEY_DIM // 2), (PEER_KEY_DIM // 2) ** -0.5),
        'peer_u': nrm(ks[17], (DEPTH, PEER_EXPERTS, D_MODEL), D_MODEL ** -0.5),
        'peer_v': nrm(ks[18], (DEPTH, PEER_EXPERTS, D_MODEL), PEER_HEADS ** -0.5),
    }


def reference(x, c, w_ada, b_ada, norm1_g, norm2_g, w_in, w_out, w_pool, pool_scale,
              q_norm_g, k_norm_g, cmp_pe, cmp_w1, cmp_w2, w_pq, peer_keys, peer_u, peer_v):
    for l in range(DEPTH):
        x = hybrid_layer(x, c, w_ada[l], b_ada[l], norm1_g[l], norm2_g[l], w_in[l], w_out[l],
                         w_pool[l], pool_scale[l], q_norm_g[l], k_norm_g[l], cmp_pe[l], cmp_w1[l],
                         cmp_w2[l], w_pq[l], peer_keys[l], peer_u[l], peer_v[l])
    return x
```

```python
import functools
import math

import numpy as np
import jax
import jax.numpy as jnp
from jax import lax
from jax.experimental import pallas as pl
from jax.experimental.pallas import tpu as pltpu

F32 = jnp.float32
BF16 = jnp.bfloat16

HEAD_DIM = 128
POOL_WINDOWS = (2, 4, 8, 16)
NSA_GROUP = 6
NSA_KV_HEADS = 4
N_BRANCH = 3
CMP_BLOCK = 32
CMP_STRIDE = 16
SEL_BLOCK = 64
SEL_TOPN = 16
WINDOW = 512
FORCE_BONUS = 1000.0
NEG_SCORE = -1e9
ROPE_THETA = 10000.0
PEER_HEADS = 8
PEER_NKEYS = 128
PEER_TOPK = 16
NORM_EPS = 1e-6
N_MOD = 6

V7X_VMEM_BYTES = 64 * 1024 * 1024
VMEM_LIMIT = V7X_VMEM_BYTES - 6 * 1024 * 1024
LANES = 128
MASKED = -1e30


def _params(*sem):
    return pltpu.CompilerParams(dimension_semantics=sem, vmem_limit_bytes=VMEM_LIMIT)


def _pick(n, pref):
    t = min(n, pref)
    while n % t:
        t //= 2
    return t


def _gelu(x):
    return 0.5 * x * (1.0 + lax.erf(x * (1.0 / math.sqrt(2.0))))


def _ada_kernel(c_ref, w_ref, b_ref, o_ref):
    c = c_ref[...]
    a = (c * jax.nn.sigmoid(c)).astype(BF16)
    o_ref[...] = jnp.dot(a, w_ref[...].astype(BF16), preferred_element_type=F32) + b_ref[...]


def _ada(c, w_ada, b_ada):
    bsz, d = c.shape
    n = w_ada.shape[1]
    tn = _pick(n, 512)
    return pl.pallas_call(
        _ada_kernel,
        grid=(n // tn,),
        in_specs=[pl.BlockSpec((bsz, d), lambda j: (0, 0)),
                  pl.BlockSpec((d, tn), lambda j: (0, j)),
                  pl.BlockSpec((1, tn), lambda j: (0, j))],
        out_specs=pl.BlockSpec((bsz, tn), lambda j: (0, j)),
        out_shape=jax.ShapeDtypeStruct((bsz, n), F32),
        compiler_params=_params("parallel"),
        name="ada",
    )(c, w_ada, b_ada.reshape(1, n))


def _modnorm_kernel(x_ref, g_ref, sc_ref, sh_ref, o_ref, *, transpose):
    x = x_ref[0]
    ms = jnp.mean(x * x, axis=-1, keepdims=True)
    y = x * lax.rsqrt(ms + NORM_EPS) * g_ref[...]
    y = y * (1.0 + sc_ref[0]) + sh_ref[0]
    if transpose:
        o_ref[...] = y.T.astype(BF16)
    else:
        o_ref[0] = y.astype(BF16)


def _modnorm(x, g, scale, shift, transpose):
    bsz, t, d = x.shape
    tt = _pick(t, 256)
    nt = t // tt
    if transpose:
        out_spec = pl.BlockSpec((d, tt), lambda b, i: (0, b * nt + i))
        out_shape = jax.ShapeDtypeStruct((d, bsz * t), BF16)
    else:
        out_spec = pl.BlockSpec((1, tt, d), lambda b, i: (b, i, 0))
        out_shape = jax.ShapeDtypeStruct((bsz, t, d), BF16)
    return pl.pallas_call(
        functools.partial(_modnorm_kernel, transpose=transpose),
        grid=(bsz, nt),
        in_specs=[pl.BlockSpec((1, tt, d), lambda b, i: (b, i, 0)),
                  pl.BlockSpec((1, d), lambda b, i: (0, 0)),
                  pl.BlockSpec((1, 1, d), lambda b, i: (b, 0, 0)),
                  pl.BlockSpec((1, 1, d), lambda b, i: (b, 0, 0))],
        out_specs=out_spec,
        out_shape=out_shape,
        compiler_params=_params("parallel", "parallel"),
        name="modnorm_t" if transpose else "modnorm",
    )(x, g.reshape(1, d), scale.reshape(bsz, 1, d), shift.reshape(bsz, 1, d))


def _mm_kernel(a_ref, w_ref, o_ref):
    o_ref[...] = jnp.dot(a_ref[...], w_ref[...], preferred_element_type=F32).astype(o_ref.dtype)


def _matmul(a, w, out_dtype, name):
    m, k = a.shape
    n = w.shape[1]
    tm = _pick(m, 1024)
    tn = _pick(n, 1024)
    return pl.pallas_call(
        _mm_kernel,
        grid=(m // tm, n // tn),
        in_specs=[pl.BlockSpec((tm, k), lambda i, j: (i, 0)),
                  pl.BlockSpec((k, tn), lambda i, j: (0, j))],
        out_specs=pl.BlockSpec((tm, tn), lambda i, j: (i, j)),
        out_shape=jax.ShapeDtypeStruct((m, n), out_dtype),
        compiler_params=_params("parallel", "parallel"),
        name=name,
    )(a, w)


def _pool_kernel(p_ref, w_ref, s_ref, o_ref):
    t = p_ref.shape[0]
    gw = w_ref.shape[1]
    row = lax.broadcasted_iota(jnp.int32, (t, 1), 0)

    def shifted(v, k):
        return jnp.where(row >= k, pltpu.roll(v, k, axis=0), 0.0)

    for gi, win in enumerate(POOL_WINDOWS):
        p = p_ref[:, gi * gw:(gi + 1) * gw].astype(F32)
        acc = p
        span = 1
        while span < win:
            acc = acc + shifted(acc, span)
            span *= 2
        count = jnp.minimum(row + 1, win).astype(F32)
        dlt = acc / count - p
        y = jnp.dot(dlt.astype(BF16), w_ref[gi], preferred_element_type=F32)
        o_ref[:, gi * gw:(gi + 1) * gw] = (y * s_ref[:, gi * gw:(gi + 1) * gw]).astype(o_ref.dtype)


def _pool(proj, w_pool, pool_scale, bsz, t):
    ng, gw, _ = w_pool.shape
    pw = ng * gw
    return pl.pallas_call(
        _pool_kernel,
        grid=(bsz,),
        in_specs=[pl.BlockSpec((t, pw), lambda b: (b, 0)),
                  pl.BlockSpec((ng, gw, gw), lambda b: (0, 0, 0)),
                  pl.BlockSpec((1, pw), lambda b: (0, 0))],
        out_specs=pl.BlockSpec((t, pw), lambda b: (b, 0)),
        out_shape=jax.ShapeDtypeStruct((bsz * t, pw), BF16),
        compiler_params=_params("parallel"),
        name="pool",
    )(proj, w_pool.astype(BF16), pool_scale.reshape(1, pw))


def _headnorm_kernel(x_ref, g_ref, cos_ref, sin_ref, *o_refs, n_heads, out_scale, with_plain):
    cosf = cos_ref[...]
    sinf = sin_ref[...]
    g = g_ref[0]
    for h in range(n_heads):
        sl = slice(h * HEAD_DIM, (h + 1) * HEAD_DIM)
        x = x_ref[:, sl].astype(F32)
        ms = jnp.mean(x * x, axis=-1, keepdims=True)
        y = x * lax.rsqrt(ms + NORM_EPS) * g
        rot = y * cosf + pltpu.roll(y, HEAD_DIM // 2, axis=1) * sinf
        if with_plain:
            o_refs[0][:, sl] = (y * out_scale).astype(BF16)
            o_refs[1][:, sl] = (rot * out_scale).astype(BF16)
        else:
            o_refs[0][:, sl] = (rot * out_scale).astype(BF16)


def _rope_tables(t):
    half = HEAD_DIM // 2
    inv = ROPE_THETA ** (-np.arange(half, dtype=np.float64) / half)
    ang = np.arange(t, dtype=np.float64)[:, None] * inv[None, :]
    cosf = np.concatenate([np.cos(ang), np.cos(ang)], axis=-1)
    sinf = np.concatenate([-np.sin(ang), np.sin(ang)], axis=-1)
    return jnp.asarray(cosf, F32), jnp.asarray(sinf, F32)


def _headnorm(proj, gamma, t, width, col_of, n_col_blocks, gamma_of, out_scale, with_plain, name):
    n = proj.shape[0]
    tt = _pick(t, 512)
    nt = t // tt
    cosf, sinf = _rope_tables(t)
    n_out = 2 if with_plain else 1
    out_w = n_col_blocks * width
    out = pl.pallas_call(
        functools.partial(_headnorm_kernel, n_heads=width // HEAD_DIM, out_scale=out_scale, with_plain=with_plain),
        grid=(n // tt, n_col_blocks),
        in_specs=[pl.BlockSpec((tt, width), lambda i, j: (i, col_of(j))),
                  pl.BlockSpec((1, 1, HEAD_DIM), lambda i, j: (gamma_of(j), 0, 0)),
                  pl.BlockSpec((tt, HEAD_DIM), lambda i, j: (i % nt, 0)),
                  pl.BlockSpec((tt, HEAD_DIM), lambda i, j: (i % nt, 0))],
        out_specs=[pl.BlockSpec((tt, width), lambda i, j: (i, j))] * n_out,
        out_shape=[jax.ShapeDtypeStruct((n, out_w), BF16)] * n_out,
        compiler_params=_params("parallel", "parallel"),
        name=name,
    )(proj, gamma, cosf, sinf)
    return out if with_plain else out[0]


def _compress_kernel(x_ref, pe_ref, w1_ref, w2_ref, g_ref, o_ref, xs_ref):
    t = x_ref.shape[0]
    nblk = t // CMP_STRIDE
    kv = pl.program_id(1)
    xs_ref[...] = x_ref[...].astype(F32)
    cols = [xs_ref[pl.ds(l, nblk, stride=CMP_STRIDE), :] for l in range(CMP_STRIDE)]
    xc = jnp.concatenate(cols, axis=1)
    lo = jnp.dot((xc + pe_ref[0, 0]).astype(BF16), w1_ref[0, 0], preferred_element_type=F32)
    hi = jnp.dot((xc + pe_ref[0, 1]).astype(BF16), w1_ref[0, 1], preferred_element_type=F32)
    pre = lo + pltpu.roll(hi, nblk - 1, axis=0)
    hid = _gelu(pre)
    out = jnp.dot(hid.astype(BF16), w2_ref[0], preferred_element_type=F32)
    ms = jnp.mean(out * out, axis=-1, keepdims=True)
    normed = out * lax.rsqrt(ms + NORM_EPS) * g_ref[...]
    out = jnp.where(kv == 0, normed, out)
    row = lax.broadcasted_iota(jnp.int32, out.shape, 0)
    o_ref[0, 0, 0] = jnp.where(row < nblk - 1, out, 0.0).astype(BF16)


def _compress(proj, cmp_pe, cmp_w1, cmp_w2, k_gamma, bsz, t, col0):
    g = NSA_KV_HEADS
    nblk = t // CMP_STRIDE
    hidden = cmp_w1.shape[-1]
    halves = CMP_BLOCK // CMP_STRIDE
    w1 = cmp_w1.astype(BF16).reshape(2, halves, CMP_STRIDE * HEAD_DIM, hidden)
    pe = cmp_pe.reshape(2, halves, 1, CMP_STRIDE * HEAD_DIM)
    cb0 = col0 // HEAD_DIM
    return pl.pallas_call(
        _compress_kernel,
        grid=(bsz, 2, g),
        in_specs=[pl.BlockSpec((t, HEAD_DIM), lambda b, kv, gi: (b, cb0 + kv * g + gi)),
                  pl.BlockSpec((1, halves, 1, CMP_STRIDE * HEAD_DIM), lambda b, kv, gi: (kv, 0, 0, 0)),
                  pl.BlockSpec((1, halves, CMP_STRIDE * HEAD_DIM, hidden), lambda b, kv, gi: (kv, 0, 0, 0)),
                  pl.BlockSpec((1, hidden, HEAD_DIM), lambda b, kv, gi: (kv, 0, 0)),
                  pl.BlockSpec((1, HEAD_DIM), lambda b, kv, gi: (0, 0))],
        out_specs=pl.BlockSpec((1, 1, 1, nblk, HEAD_DIM), lambda b, kv, gi: (kv, b, gi, 0, 0)),
        out_shape=jax.ShapeDtypeStruct((2, bsz, g, nblk, HEAD_DIM), BF16),
        scratch_shapes=[pltpu.VMEM((t, HEAD_DIM), F32)],
        compiler_params=_params("parallel", "parallel", "parallel"),
        name="compress",
    )(proj, pe, w1, cmp_w2.astype(BF16), k_gamma.reshape(1, HEAD_DIM))


def _split3(x):
    a = x.astype(BF16)
    r = x - a.astype(F32)
    b = r.astype(BF16)
    c = (r - b.astype(F32)).astype(BF16)
    return a, b, c


def _cmp_attn_kernel(q_ref, kc_ref, vc_ref, map_ref, oc_ref, sel_ref, *, n_sel, top_n):
    tq = q_ref.shape[0]
    ncmp = kc_ref.shape[3]
    r = NSA_GROUP
    t0 = pl.program_id(1) * tq
    tcol = t0 + lax.broadcasted_iota(jnp.int32, (tq, 1), 0)
    tcol_r = jnp.concatenate([tcol] * r, axis=0)
    nrow = lax.broadcasted_iota(jnp.int32, (1, ncmp), 1)
    cmp_ok = (nrow * CMP_STRIDE + (CMP_BLOCK - 1) <= tcol_r) & (nrow < ncmp - 1)
    jrow = lax.broadcasted_iota(jnp.int32, (1, ncmp), 1)
    cur = tcol // SEL_BLOCK
    forced = (jrow == 0) | (jrow == cur) | (jrow == cur - 1)
    valid = jrow * SEL_BLOCK <= tcol
    jsub = lax.broadcasted_iota(jnp.int32, (n_sel, tq), 0)
    sel_rows = []
    for g in range(NSA_KV_HEADS):
        q = jnp.concatenate(
            [q_ref[:, (g * r + i) * HEAD_DIM:(g * r + i + 1) * HEAD_DIM] for i in range(r)], axis=0)
        s = lax.dot_general(q, kc_ref[0, 0, g], (((1,), (1,)), ((), ())), preferred_element_type=F32)
        s = jnp.where(cmp_ok, s, MASKED)
        m = jnp.max(s, axis=-1, keepdims=True)
        p = jnp.where(cmp_ok, jnp.exp(s - m), 0.0)
        den = jnp.sum(p, axis=-1, keepdims=True)
        p = p / jnp.where(den > 0, den, 1.0)
        o = jnp.dot(p.astype(BF16), vc_ref[0, 0, g], preferred_element_type=F32)
        for i in range(r):
            oc_ref[:, (g * r + i) * HEAD_DIM:(g * r + i + 1) * HEAD_DIM] = o[i * tq:(i + 1) * tq].astype(oc_ref.dtype)
        psum = p[0:tq]
        for i in range(1, r):
            psum = psum + p[i * tq:(i + 1) * tq]
        imp = sum(jnp.dot(piece, map_ref[...], preferred_element_type=F32) for piece in _split3(psum))
        score = jnp.where(valid, imp + jnp.where(forced, FORCE_BONUS, 0.0), NEG_SCORE)
        st = score.T[0:n_sel]
        rank = jnp.zeros((n_sel, tq), F32)
        for i in range(n_sel):
            ri = st[i:i + 1]
            beats = jnp.where(ri > st, 1.0, jnp.where((ri == st) & (jsub > i), 1.0, 0.0))
            rank = rank + beats
        sel_rows.append(jnp.where(rank < top_n, 1.0, 0.0))
    pad = LANES - NSA_KV_HEADS * n_sel
    if pad:
        sel_rows.append(jnp.zeros((pad, tq), F32))
    sel_ref[...] = jnp.concatenate(sel_rows, axis=0).T.astype(sel_ref.dtype)


def _sel_map(ncmp_pad, n_sel):
    cst = np.arange(ncmp_pad)[:, None] * CMP_STRIDE
    sst = np.arange(LANES)[None, :] * SEL_BLOCK
    ov = np.clip(np.minimum(cst + CMP_BLOCK, sst + SEL_BLOCK) - np.maximum(cst, sst), 0, None) / CMP_BLOCK
    ov[:, n_sel:] = 0.0
    ov[ncmp_pad - 1:, :] = 0.0
    return jnp.asarray(ov, BF16)


def _cmp_attn(qn, kvc, bsz, t):
    n, qw = qn.shape
    ncmp = kvc.shape[3]
    n_sel = t // SEL_BLOCK
    top_n = min(SEL_TOPN, n_sel)
    assert NSA_KV_HEADS * n_sel <= LANES and ncmp <= LANES
    tq = _pick(t, 256)
    nq = t // tq
    kv_spec = lambda kv: pl.BlockSpec((1, 1, NSA_KV_HEADS, ncmp, HEAD_DIM), lambda b, i: (kv, b, 0, 0, 0))
    return pl.pallas_call(
        functools.partial(_cmp_attn_kernel, n_sel=n_sel, top_n=top_n),
        grid=(bsz, nq),
        in_specs=[pl.BlockSpec((tq, qw), lambda b, i: (b * nq + i, 0)),
                  kv_spec(0), kv_spec(1),
                  pl.BlockSpec((ncmp, LANES), lambda b, i: (0, 0))],
        out_specs=[pl.BlockSpec((tq, qw), lambda b, i: (b * nq + i, 0)),
                   pl.BlockSpec((tq, LANES), lambda b, i: (b * nq + i, 0))],
        out_shape=[jax.ShapeDtypeStruct((n, qw), BF16), jax.ShapeDtypeStruct((n, LANES), BF16)],
        compiler_params=_params("parallel", "parallel"),
        name="cmp_attn",
    )(qn, kvc, kvc, _sel_map(ncmp, n_sel))


def _flash_branch(q, k_ref, v_ref, lo, hi, tk, mask_fn):
    m_rows = q.shape[0]

    def body(j, carry):
        m, l, acc = carry
        start = pl.multiple_of(j * tk, tk)
        k = k_ref[pl.ds(start, tk), :]
        v = v_ref[pl.ds(start, tk), :]
        s = lax.dot_general(q, k, (((1,), (1,)), ((), ())), preferred_element_type=F32)
        ok = mask_fn(j)
        s = jnp.where(ok, s, MASKED)
        m_new = jnp.maximum(m, jnp.max(s, axis=-1, keepdims=True))
        alpha = jnp.exp(m - m_new)
        p = jnp.where(ok, jnp.exp(s - m_new), 0.0)
        l = alpha * l + jnp.sum(p, axis=-1, keepdims=True)
        acc = alpha * acc + jnp.dot(p.astype(BF16), v, preferred_element_type=F32)
        return m_new, l, acc

    init = (jnp.full((m_rows, 1), MASKED, F32), jnp.zeros((m_rows, 1), F32), jnp.zeros((m_rows, HEAD_DIM), F32))
    _, l, acc = lax.fori_loop(lo, hi, body, init)
    return acc / jnp.where(l > 0, l, 1.0)


def _sw_attn_kernel(q_ref, ks_ref, vs_ref, kw_ref, vw_ref, sel_ref, oc_ref, gate_ref, o_ref, *, tk, n_sel):
    tq = q_ref.shape[0]
    r = NSA_GROUP
    g = pl.program_id(1)
    t0 = pl.program_id(2) * tq
    q = jnp.concatenate([q_ref[:, i * HEAD_DIM:(i + 1) * HEAD_DIM] for i in range(r)], axis=0)
    tcol = t0 + lax.broadcasted_iota(jnp.int32, (tq, 1), 0)
    tcol_r = jnp.concatenate([tcol] * r, axis=0)
    sel = sel_ref[...]
    lane = lax.broadcasted_iota(jnp.int32, (1, tk), 1)
    crow = lax.broadcasted_iota(jnp.int32, (LANES, 1), 0)

    def sel_mask(j):
        kpos = j * tk + lane
        expand = jnp.where(crow == g * n_sel + kpos // SEL_BLOCK, 1.0, 0.0).astype(BF16)
        chosen = jnp.dot(sel, expand, preferred_element_type=F32)
        chosen = jnp.concatenate([chosen] * r, axis=0)
        return jnp.where(kpos <= tcol_r, chosen, 0.0) > 0.5

    def win_mask(j):
        dist = tcol_r - (j * tk + lane)
        return (dist >= 0) & (dist < WINDOW)

    hi = (t0 + tq + tk - 1) // tk
    o_s = _flash_branch(q, ks_ref, vs_ref, 0, hi, tk, sel_mask)
    lo_w = jnp.maximum(t0 - (WINDOW - 1), 0) // tk
    o_w = _flash_branch(q, kw_ref, vw_ref, lo_w, hi, tk, win_mask)
    gate = jax.nn.sigmoid(gate_ref[...])
    for i in range(r):
        rows = slice(i * tq, (i + 1) * tq)
        cols = slice(i * HEAD_DIM, (i + 1) * HEAD_DIM)
        o = (gate[:, 3 * i:3 * i + 1] * oc_ref[:, cols].astype(F32)
             + gate[:, 3 * i + 1:3 * i + 2] * o_s[rows]
             + gate[:, 3 * i + 2:3 * i + 3] * o_w[rows])
        o_ref[:, cols] = o.astype(o_ref.dtype)


def _sw_attn(qr, kr, proj, sel, o_c, gates, bsz, t, vsel_col, vwin_col):
    n, qw = qr.shape
    g = NSA_KV_HEADS
    gw = NSA_GROUP * HEAD_DIM
    tq = _pick(t, 256)
    tk = tq
    nq = t // tq
    n_sel = t // SEL_BLOCK
    vs0, vw0 = vsel_col // HEAD_DIM, vwin_col // HEAD_DIM
    qspec = pl.BlockSpec((tq, gw), lambda b, gi, i: (b * nq + i, gi))
    return pl.pallas_call(
        functools.partial(_sw_attn_kernel, tk=tk, n_sel=n_sel),
        grid=(bsz, g, nq),
        in_specs=[qspec,
                  pl.BlockSpec((t, HEAD_DIM), lambda b, gi, i: (b, gi)),
                  pl.BlockSpec((t, HEAD_DIM), lambda b, gi, i: (b, vs0 + gi)),
                  pl.BlockSpec((t, HEAD_DIM), lambda b, gi, i: (b, g + gi)),
                  pl.BlockSpec((t, HEAD_DIM), lambda b, gi, i: (b, vw0 + gi)),
                  pl.BlockSpec((tq, LANES), lambda b, gi, i: (b * nq + i, 0)),
                  qspec,
                  pl.BlockSpec((tq, LANES), lambda b, gi, i: (b * nq + i, gi))],
        out_specs=qspec,
        out_shape=jax.ShapeDtypeStruct((n, qw), BF16),
        compiler_params=_params("parallel", "parallel", "parallel"),
        name="sw_attn",
    )(qr, kr, proj, kr, proj, sel, o_c, gates)


def _outproj_kernel(a1_ref, a2_ref, w1_ref, w2_ref, x_ref, g_ref, o_ref):
    y = jnp.dot(a1_ref[...], w1_ref[...], preferred_element_type=F32)
    y = y + jnp.dot(a2_ref[...], w2_ref[...], preferred_element_type=F32)
    o_ref[...] = x_ref[...] + g_ref[0] * y


def _outproj(y_pool, y_nsa, w_out, x2d, gate, bsz, t):
    n, d = x2d.shape
    k1, k2 = y_pool.shape[1], y_nsa.shape[1]
    tm = _pick(t, 1024)
    tn = _pick(d, 1024)
    nt = t // tm
    w = w_out.astype(BF16)
    return pl.pallas_call(
        _outproj_kernel,
        grid=(n // tm, d // tn),
        in_specs=[pl.BlockSpec((tm, k1), lambda i, j: (i, 0)),
                  pl.BlockSpec((tm, k2), lambda i, j: (i, 0)),
                  pl.BlockSpec((k1, tn), lambda i, j: (0, j)),
                  pl.BlockSpec((k2, tn), lambda i, j: (0, j)),
                  pl.BlockSpec((tm, tn), lambda i, j: (i, j)),
                  pl.BlockSpec((1, 1, tn), lambda i, j: (i // nt, 0, j))],
        out_specs=pl.BlockSpec((tm, tn), lambda i, j: (i, j)),
        out_shape=jax.ShapeDtypeStruct((n, d), F32),
        compiler_params=_params("parallel", "parallel"),
        name="outproj",
    )(y_pool, y_nsa, w[:k1], w[k1:], x2d, gate.reshape(bsz, 1, d))


def _cmpx(v, i, j):
    hi = jnp.maximum(v[i], v[j])
    lo = jnp.minimum(v[i], v[j])
    v[i], v[j] = hi, lo


def _bitonic_merge_desc(v):
    n = len(v)
    d = n // 2
    while d >= 1:
        for i in range(n):
            if (i // d) % 2 == 0:
                _cmpx(v, i, i + d)
        d //= 2


def _sort_desc(v):
    n = len(v)
    size = 2
    while size <= n:
        for base in range(0, n, size):
            blk = v[base:base + size]
            blk[size // 2:] = blk[size // 2:][::-1]
            _bitonic_merge_desc(blk)
            v[base:base + size] = blk
        size *= 2


def _merge_top(a, b):
    n = len(a)
    v = [jnp.maximum(a[i], b[n - 1 - i]) for i in range(n)]
    _bitonic_merge_desc(v)
    return v


def _top_sublanes(s, k):
    nk = s.shape[0]
    groups = nk // 8
    assert groups == k
    v = [s[8 * i:8 * i + 8] for i in range(groups)]
    _sort_desc(v)
    shift = 4
    while shift >= 1:
        v = _merge_top(v, [pltpu.roll(x, shift, axis=0) for x in v])
        shift //= 2
    return v


def _route_kernel(qt_ref, keys_ref, st_ref, stat_ref):
    k = PEER_TOPK
    nh = keys_ref.shape[0]
    kd = keys_ref.shape[3]
    neg = jnp.full((8, qt_ref.shape[1]), -jnp.inf, F32)
    thr_rows, m0_rows, m1_rows, iz_rows = [], [], [], []
    for h in range(nh):
        tops = []
        for c in range(2):
            row0 = (h * 2 + c) * kd
            qa, qb, qc = _split3(qt_ref[row0:row0 + kd, :])
            ka, kb, _ = _split3(keys_ref[h, c])
            s = (jnp.dot(ka, qa, preferred_element_type=F32) + jnp.dot(ka, qb, preferred_element_type=F32)
                 + jnp.dot(kb, qa, preferred_element_type=F32) + jnp.dot(ka, qc, preferred_element_type=F32)
                 + jnp.dot(kb, qb, preferred_element_type=F32))
            st_ref[(h * 2 + c) * PEER_NKEYS:(h * 2 + c + 1) * PEER_NKEYS, :] = s
            tops.append(_top_sublanes(s, k))
        v0, v1 = tops
        cands = [v0[a] + v1[b] for a in range(k) for b in range(k) if (a + 1) * (b + 1) <= k]
        lists = []
        for base in range(0, len(cands), k):
            blk = cands[base:base + k]
            blk = blk + [neg] * (k - len(blk))
            _sort_desc(blk)
            lists.append(blk)
        best = lists[0]
        for other in lists[1:]:
            best = _merge_top(best, other)
        z = sum(jnp.exp(b - best[0]) for b in best)
        thr_rows.append(best[k - 1][0:1])
        m0_rows.append(v0[0][0:1])
        m1_rows.append(v1[0][0:1])
        iz_rows.append((1.0 / z)[0:1])
    stat_ref[...] = jnp.concatenate(thr_rows + m0_rows + m1_rows + iz_rows, axis=0)


def _route(qt, peer_keys):
    rows, n = qt.shape
    nh, _, nk, kd = peer_keys.shape
    assert nk == PEER_NKEYS and nk == 8 * PEER_TOPK
    tr = _pick(n, 512)
    return pl.pallas_call(
        _route_kernel,
        grid=(n // tr,),
        in_specs=[pl.BlockSpec((rows, tr), lambda i: (0, i)),
                  pl.BlockSpec((nh, 2, nk, kd), lambda i: (0, 0, 0, 0))],
        out_specs=[pl.BlockSpec((nh * 2 * nk, tr), lambda i: (0, i)),
                   pl.BlockSpec((4 * nh, tr), lambda i: (0, i))],
        out_shape=[jax.ShapeDtypeStruct((nh * 2 * nk, n), F32), jax.ShapeDtypeStruct((4 * nh, n), F32)],
        compiler_params=_params("parallel"),
        name="route",
    )(qt, peer_keys)


def _peer_kernel(ht_ref, u_ref, v_ref, st_ref, stat_ref, o_ref, e0_ref, e1_ref):
    j = pl.program_id(1)
    te = u_ref.shape[0]
    nk = PEER_NKEYS
    nh = e0_ref.shape[0]
    per_step = te // nk

    @pl.when(j == 0)
    def _():
        o_ref[...] = jnp.zeros_like(o_ref)
        for h in range(nh):
            m0 = stat_ref[nh + h:nh + h + 1, :]
            m1 = stat_ref[2 * nh + h:2 * nh + h + 1, :]
            iz = stat_ref[3 * nh + h:3 * nh + h + 1, :]
            e0_ref[h] = jnp.exp(st_ref[(2 * h) * nk:(2 * h + 1) * nk, :] - m0)
            e1_ref[h] = jnp.exp(st_ref[(2 * h + 1) * nk:(2 * h + 2) * nk, :] - m1) * iz

    act = _gelu(jnp.dot(u_ref[...], ht_ref[...], preferred_element_type=F32))
    slabs = []
    for a in range(per_step):
        i0 = j * per_step + a
        w = None
        for h in range(nh):
            s0 = st_ref[pl.ds((2 * h) * nk + i0, 1), :]
            e0 = e0_ref[h, pl.ds(i0, 1), :]
            ssum = s0 + st_ref[(2 * h + 1) * nk:(2 * h + 2) * nk, :]
            wh = jnp.where(ssum >= stat_ref[h:h + 1, :], e0 * e1_ref[h], 0.0)
            w = wh if w is None else w + wh
        slabs.append((w * act[a * nk:(a + 1) * nk]).astype(BF16))
    wa = jnp.concatenate(slabs, axis=0)
    o_ref[...] += lax.dot_general(wa, v_ref[...], (((0,), (0,)), ((), ())), preferred_element_type=F32)


def _peer(ht, u, v, st, stats):
    d, n = ht.shape
    e = u.shape[0]
    nh = stats.shape[0] // 4
    tm = _pick(n, 512)
    te = _pick(e, 512)
    return pl.pallas_call(
        _peer_kernel,
        grid=(n // tm, e // te),
        in_specs=[pl.BlockSpec((d, tm), lambda i, j: (0, i)),
                  pl.BlockSpec((te, d), lambda i, j: (j, 0)),
                  pl.BlockSpec((te, d), lambda i, j: (j, 0)),
                  pl.BlockSpec((st.shape[0], tm), lambda i, j: (0, i)),
                  pl.BlockSpec((stats.shape[0], tm), lambda i, j: (0, i))],
        out_specs=pl.BlockSpec((tm, d), lambda i, j: (i, 0)),
        out_shape=jax.ShapeDtypeStruct((n, d), F32),
        scratch_shapes=[pltpu.VMEM((nh, PEER_NKEYS, tm), F32), pltpu.VMEM((nh, PEER_NKEYS, tm), F32)],
        compiler_params=_params("parallel", "arbitrary"),
        name="peer",
    )(ht, u, v, st, stats)


def _combine_kernel(x_ref, y_ref, g_ref, o_ref):
    o_ref[0] = x_ref[0] + g_ref[0] * y_ref[0]


def _combine(x, y, gate):
    bsz, t, d = x.shape
    tt = _pick(t, 256)
    spec = pl.BlockSpec((1, tt, d), lambda b, i: (b, i, 0))
    return pl.pallas_call(
        _combine_kernel,
        grid=(bsz, t // tt),
        in_specs=[spec, spec, pl.BlockSpec((1, 1, d), lambda b, i: (b, 0, 0))],
        out_specs=spec,
        out_shape=jax.ShapeDtypeStruct((bsz, t, d), F32),
        compiler_params=_params("parallel", "parallel"),
        name="combine",
    )(x, y, gate.reshape(bsz, 1, d))


def _layer(x, c, w_ada, b_ada, norm1_g, norm2_g, w_in, w_out, w_pool, pool_scale, q_norm_g, k_norm_g,
           cmp_pe, cmp_w1, cmp_w2, w_pq, peer_keys, peer_u, peer_v):
    bsz, t, d = x.shape
    n = bsz * t
    g, r = NSA_KV_HEADS, NSA_GROUP
    pool_w = w_pool.shape[0] * w_pool.shape[1]
    nsa_w = g * r * HEAD_DIM
    kv_w = N_BRANCH * 2 * g * HEAD_DIM
    gate_w = g * r * N_BRANCH
    assert w_in.shape[1] == pool_w + nsa_w + kv_w + gate_w
    q_col, kv_col = pool_w, pool_w + nsa_w
    branch_w = g * HEAD_DIM
    assert pool_w % nsa_w in (0, pool_w) and q_col % 1024 == 0

    mod = _ada(c, w_ada, b_ada)
    shift1, scale1, gate1, shift2, scale2, gate2 = [mod[:, i * d:(i + 1) * d] for i in range(N_MOD)]

    h = _modnorm(x, norm1_g, scale1, shift1, transpose=False).reshape(n, d)
    main_w = pool_w + nsa_w + kv_w
    w_main = w_in[:, :main_w].astype(BF16)
    w_gate = w_in[:, main_w:].reshape(d, g, r * N_BRANCH)
    w_gate = jnp.pad(w_gate, ((0, 0), (0, 0), (0, LANES - r * N_BRANCH))).reshape(d, g * LANES).astype(BF16)
    proj = _matmul(h, w_main, BF16, "inproj")
    gates = _matmul(h, w_gate, F32, "gateproj")

    y_pool = _pool(proj, w_pool, pool_scale, bsz, t)

    scale = HEAD_DIM ** -0.5
    qblk = 1024
    qn, qr = _headnorm(proj, q_norm_g.reshape(1, 1, HEAD_DIM), t, qblk, lambda j: q_col // qblk + j, nsa_w // qblk,
                       lambda j: 0, scale, True, "qprep")
    ksel_blk = (kv_col + 2 * branch_w) // branch_w
    kr = _headnorm(proj, k_norm_g.reshape(N_BRANCH, 1, HEAD_DIM), t, branch_w, lambda j: ksel_blk + 2 * j, 2,
                   lambda j: 1 + j, 1.0, False, "kprep")
    kvc = _compress(proj, cmp_pe, cmp_w1, cmp_w2, k_norm_g[0], bsz, t, kv_col)
    o_c, sel = _cmp_attn(qn, kvc, bsz, t)
    y_nsa = _sw_attn(qr, kr, proj, sel, o_c, gates, bsz, t, kv_col + 3 * branch_w, kv_col + 5 * branch_w)

    x1 = _outproj(y_pool, y_nsa, w_out, x.reshape(n, d), gate1, bsz, t)

    ht = _modnorm(x1.reshape(bsz, t, d), norm2_g, scale2, shift2, transpose=True)
    qt = _matmul(w_pq.T.astype(BF16), ht, F32, "peer_q")
    st, stats = _route(qt, peer_keys)
    y = _peer(ht, peer_u.astype(BF16), peer_v.astype(BF16), st, stats)
    return _combine(x1.reshape(bsz, t, d), y.reshape(bsz, t, d), gate2)


def kernel(x, c, w_ada, b_ada, norm1_g, norm2_g, w_in, w_out, w_pool, pool_scale, q_norm_g, k_norm_g, cmp_pe,
           cmp_w1, cmp_w2, w_pq, peer_keys, peer_u, peer_v):
    for l in range(w_ada.shape[0]):
        x = _layer(x, c, w_ada[l], b_ada[l], norm1_g[l], norm2_g[l], w_in[l], w_out[l], w_pool[l], pool_scale[l],
                   q_norm_g[l], k_norm_g[l], cmp_pe[l], cmp_w1[l], cmp_w2[l], w_pq[l], peer_keys[l], peer_u[l],
                   peer_v[l])
    return x
```

```python
import functools
import math

import numpy as np
import jax
import jax.numpy as jnp
from jax import lax
from jax.experimental import pallas as pl
from jax.experimental.pallas import tpu as pltpu

F32 = jnp.float32
BF16 = jnp.bfloat16

HEAD_DIM = 128
POOL_WINDOWS = (2, 4, 8, 16)
NSA_GROUP = 6
NSA_KV_HEADS = 4
N_BRANCH = 3
CMP_BLOCK = 32
CMP_STRIDE = 16
SEL_BLOCK = 64
SEL_TOPN = 16
WINDOW = 512
FORCE_BONUS = 1000.0
NEG_SCORE = -1e9
ROPE_THETA = 10000.0
PEER_HEADS = 8
PEER_NKEYS = 128
PEER_TOPK = 16
NORM_EPS = 1e-6
N_MOD = 6

V7X_VMEM_BYTES = 64 * 1024 * 1024
VMEM_LIMIT = V7X_VMEM_BYTES - 6 * 1024 * 1024
LANES = 128
MASKED = -1e30


def _params(*sem):
    return pltpu.CompilerParams(dimension_semantics=sem, vmem_limit_bytes=VMEM_LIMIT)


def _pick(n, pref):
    t = min(n, pref)
    while n % t:
        t //= 2
    return t


def _gelu(x):
    return 0.5 * x * (1.0 + lax.erf(x * (1.0 / math.sqrt(2.0))))


def _ada_kernel(c_ref, w_ref, b_ref, o_ref):
    c = c_ref[...]
    a = (c * jax.nn.sigmoid(c)).astype(BF16)
    o_ref[...] = jnp.dot(a, w_ref[...].astype(BF16), preferred_element_type=F32) + b_ref[...]


def _ada(c, w_ada, b_ada):
    bsz, d = c.shape
    n = w_ada.shape[1]
    tn = _pick(n, 512)
    return pl.pallas_call(
        _ada_kernel,
        grid=(n // tn,),
        in_specs=[pl.BlockSpec((bsz, d), lambda j: (0, 0)),
                  pl.BlockSpec((d, tn), lambda j: (0, j)),
                  pl.BlockSpec((1, tn), lambda j: (0, j))],
        out_specs=pl.BlockSpec((bsz, tn), lambda j: (0, j)),
        out_shape=jax.ShapeDtypeStruct((bsz, n), F32),
        compiler_params=_params("parallel"),
        name="ada",
    )(c, w_ada, b_ada.reshape(1, n))


def _modnorm_kernel(x_ref, g_ref, sc_ref, sh_ref, o_ref, *, transpose):
    x = x_ref[0]
    ms = jnp.mean(x * x, axis=-1, keepdims=True)
    y = x * lax.rsqrt(ms + NORM_EPS) * g_ref[...]
    y = y * (1.0 + sc_ref[0]) + sh_ref[0]
    if transpose:
        o_ref[...] = y.T.astype(BF16)
    else:
        o_ref[0] = y.astype(BF16)


def _modnorm(x, g, scale, shift, transpose):
    bsz, t, d = x.shape
    tt = _pick(t, 256)
    nt = t // tt
    if transpose:
        out_spec = pl.BlockSpec((d, tt), lambda b, i: (0, b * nt + i))
        out_shape = jax.ShapeDtypeStruct((d, bsz * t), BF16)
    else:
        out_spec = pl.BlockSpec((1, tt, d), lambda b, i: (b, i, 0))
        out_shape = jax.ShapeDtypeStruct((bsz, t, d), BF16)
    return pl.pallas_call(
        functools.partial(_modnorm_kernel, transpose=transpose),
        grid=(bsz, nt),
        in_specs=[pl.BlockSpec((1, tt, d), lambda b, i: (b, i, 0)),
                  pl.BlockSpec((1, d), lambda b, i: (0, 0)),
                  pl.BlockSpec((1, 1, d), lambda b, i: (b, 0, 0)),
                  pl.BlockSpec((1, 1, d), lambda b, i: (b, 0, 0))],
        out_specs=out_spec,
        out_shape=out_shape,
        compiler_params=_params("parallel", "parallel"),
        name="modnorm_t" if transpose else "modnorm",
    )(x, g.reshape(1, d), scale.reshape(bsz, 1, d), shift.reshape(bsz, 1, d))


def _mm_kernel(a_ref, w_ref, o_ref):
    o_ref[...] = jnp.dot(a_ref[...], w_ref[...], preferred_element_type=F32).astype(o_ref.dtype)


def _matmul(a, w, out_dtype, name):
    m, k = a.shape
    n = w.shape[1]
    tm = _pick(m, 1024)
    tn = _pick(n, 1024)
    return pl.pallas_call(
        _mm_kernel,
        grid=(m // tm, n // tn),
        in_specs=[pl.BlockSpec((tm, k), lambda i, j: (i, 0)),
                  pl.BlockSpec((k, tn), lambda i, j: (0, j))],
        out_specs=pl.BlockSpec((tm, tn), lambda i, j: (i, j)),
        out_shape=jax.ShapeDtypeStruct((m, n), out_dtype),
        compiler_params=_params("parallel", "parallel"),
        name=name,
    )(a, w)


def _pool_kernel(p_ref, w_ref, s_ref, o_ref):
    t = p_ref.shape[0]
    gw = w_ref.shape[1]
    row = lax.broadcasted_iota(jnp.int32, (t, 1), 0)

    def shifted(v, k):
        return jnp.where(row >= k, pltpu.roll(v, k, axis=0), 0.0)

    for gi, win in enumerate(POOL_WINDOWS):
        p = p_ref[:, gi * gw:(gi + 1) * gw].astype(F32)
        acc = p
        span = 1
        while span < win:
            acc = acc + shifted(acc, span)
            span *= 2
        count = jnp.minimum(row + 1, win).astype(F32)
        dlt = acc / count - p
        y = jnp.dot(dlt.astype(BF16), w_ref[gi], preferred_element_type=F32)
        o_ref[:, gi * gw:(gi + 1) * gw] = (y * s_ref[:, gi * gw:(gi + 1) * gw]).astype(o_ref.dtype)


def _pool(proj, w_pool, pool_scale, bsz, t):
    ng, gw, _ = w_pool.shape
    pw = ng * gw
    return pl.pallas_call(
        _pool_kernel,
        grid=(bsz,),
        in_specs=[pl.BlockSpec((t, pw), lambda b: (b, 0)),
                  pl.BlockSpec((ng, gw, gw), lambda b: (0, 0, 0)),
                  pl.BlockSpec((1, pw), lambda b: (0, 0))],
        out_specs=pl.BlockSpec((t, pw), lambda b: (b, 0)),
        out_shape=jax.ShapeDtypeStruct((bsz * t, pw), BF16),
        compiler_params=_params("parallel"),
        name="pool",
    )(proj, w_pool.astype(BF16), pool_scale.reshape(1, pw))


def _headnorm_kernel(x_ref, g_ref, cos_ref, sin_ref, *o_refs, n_heads, out_scale, with_plain):
    cosf = cos_ref[...]
    sinf = sin_ref[...]
    g = g_ref[0]
    for h in range(n_heads):
        sl = slice(h * HEAD_DIM, (h + 1) * HEAD_DIM)
        x = x_ref[:, sl].astype(F32)
        ms = jnp.mean(x * x, axis=-1, keepdims=True)
        y = x * lax.rsqrt(ms + NORM_EPS) * g
        rot = y * cosf + pltpu.roll(y, HEAD_DIM // 2, axis=1) * sinf
        if with_plain:
            o_refs[0][:, sl] = (y * out_scale).astype(BF16)
            o_refs[1][:, sl] = (rot * out_scale).astype(BF16)
        else:
            o_refs[0][:, sl] = (rot * out_scale).astype(BF16)


def _rope_tables(t):
    half = HEAD_DIM // 2
    inv = ROPE_THETA ** (-np.arange(half, dtype=np.float64) / half)
    ang = np.arange(t, dtype=np.float64)[:, None] * inv[None, :]
    cosf = np.concatenate([np.cos(ang), np.cos(ang)], axis=-1)
    sinf = np.concatenate([-np.sin(ang), np.sin(ang)], axis=-1)
    return jnp.asarray(cosf, F32), jnp.asarray(sinf, F32)


def _headnorm(proj, gamma, t, width, col_of, n_col_blocks, gamma_of, out_scale, with_plain, name):
    n = proj.shape[0]
    tt = _pick(t, 512)
    nt = t // tt
    cosf, sinf = _rope_tables(t)
    n_out = 2 if with_plain else 1
    out_w = n_col_blocks * width
    out = pl.pallas_call(
        functools.partial(_headnorm_kernel, n_heads=width // HEAD_DIM, out_scale=out_scale, with_plain=with_plain),
        grid=(n // tt, n_col_blocks),
        in_specs=[pl.BlockSpec((tt, width), lambda i, j: (i, col_of(j))),
                  pl.BlockSpec((1, 1, HEAD_DIM), lambda i, j: (gamma_of(j), 0, 0)),
                  pl.BlockSpec((tt, HEAD_DIM), lambda i, j: (i % nt, 0)),
                  pl.BlockSpec((tt, HEAD_DIM), lambda i, j: (i % nt, 0))],
        out_specs=[pl.BlockSpec((tt, width), lambda i, j: (i, j))] * n_out,
        out_shape=[jax.ShapeDtypeStruct((n, out_w), BF16)] * n_out,
        compiler_params=_params("parallel", "parallel"),
        name=name,
    )(proj, gamma, cosf, sinf)
    return out if with_plain else out[0]


def _compress_kernel(x_ref, pe_ref, w1_ref, w2_ref, g_ref, o_ref, xs_ref):
    t = x_ref.shape[0]
    nblk = t // CMP_STRIDE
    kv = pl.program_id(1)
    xs_ref[...] = x_ref[...].astype(F32)
    cols = [xs_ref[pl.ds(l, nblk, stride=CMP_STRIDE), :] for l in range(CMP_STRIDE)]
    xc = jnp.concatenate(cols, axis=1)
    lo = jnp.dot((xc + pe_ref[0, 0]).astype(BF16), w1_ref[0, 0], preferred_element_type=F32)
    hi = jnp.dot((xc + pe_ref[0, 1]).astype(BF16), w1_ref[0, 1], preferred_element_type=F32)
    pre = lo + pltpu.roll(hi, nblk - 1, axis=0)
    hid = _gelu(pre)
    out = jnp.dot(hid.astype(BF16), w2_ref[0], preferred_element_type=F32)
    ms = jnp.mean(out * out, axis=-1, keepdims=True)
    normed = out * lax.rsqrt(ms + NORM_EPS) * g_ref[...]
    out = jnp.where(kv == 0, normed, out)
    row = lax.broadcasted_iota(jnp.int32, out.shape, 0)
    o_ref[0, 0, 0] = jnp.where(row < nblk - 1, out, 0.0).astype(BF16)


def _compress(proj, cmp_pe, cmp_w1, cmp_w2, k_gamma, bsz, t, col0):
    g = NSA_KV_HEADS
    nblk = t // CMP_STRIDE
    hidden = cmp_w1.shape[-1]
    halves = CMP_BLOCK // CMP_STRIDE
    w1 = cmp_w1.astype(BF16).reshape(2, halves, CMP_STRIDE * HEAD_DIM, hidden)
    pe = cmp_pe.reshape(2, halves, 1, CMP_STRIDE * HEAD_DIM)
    cb0 = col0 // HEAD_DIM
    return pl.pallas_call(
        _compress_kernel,
        grid=(bsz, 2, g),
        in_specs=[pl.BlockSpec((t, HEAD_DIM), lambda b, kv, gi: (b, cb0 + kv * g + gi)),
                  pl.BlockSpec((1, halves, 1, CMP_STRIDE * HEAD_DIM), lambda b, kv, gi: (kv, 0, 0, 0)),
                  pl.BlockSpec((1, halves, CMP_STRIDE * HEAD_DIM, hidden), lambda b, kv, gi: (kv, 0, 0, 0)),
                  pl.BlockSpec((1, hidden, HEAD_DIM), lambda b, kv, gi: (kv, 0, 0)),
                  pl.BlockSpec((1, HEAD_DIM), lambda b, kv, gi: (0, 0))],
        out_specs=pl.BlockSpec((1, 1, 1, nblk, HEAD_DIM), lambda b, kv, gi: (kv, b, gi, 0, 0)),
        out_shape=jax.ShapeDtypeStruct((2, bsz, g, nblk, HEAD_DIM), BF16),
        scratch_shapes=[pltpu.VMEM((t, HEAD_DIM), F32)],
        compiler_params=_params("parallel", "parallel", "parallel"),
        name="compress",
    )(proj, pe, w1, cmp_w2.astype(BF16), k_gamma.reshape(1, HEAD_DIM))


def _split3(x):
    a = x.astype(BF16)
    r = x - a.astype(F32)
    b = r.astype(BF16)
    c = (r - b.astype(F32)).astype(BF16)
    return a, b, c


def _cmp_attn_kernel(q_ref, kc_ref, vc_ref, map_ref, oc_ref, sel_ref, *, n_sel, top_n):
    tq = q_ref.shape[0]
    ncmp = kc_ref.shape[3]
    r = NSA_GROUP
    t0 = pl.program_id(1) * tq
    tcol = t0 + lax.broadcasted_iota(jnp.int32, (tq, 1), 0)
    tcol_r = jnp.concatenate([tcol] * r, axis=0)
    nrow = lax.broadcasted_iota(jnp.int32, (1, ncmp), 1)
    cmp_ok = (nrow * CMP_STRIDE + (CMP_BLOCK - 1) <= tcol_r) & (nrow < ncmp - 1)
    jrow = lax.broadcasted_iota(jnp.int32, (1, ncmp), 1)
    cur = tcol // SEL_BLOCK
    forced = (jrow == 0) | (jrow == cur) | (jrow == cur - 1)
    valid = jrow * SEL_BLOCK <= tcol
    jsub = lax.broadcasted_iota(jnp.int32, (n_sel, tq), 0)
    sel_rows = []
    for g in range(NSA_KV_HEADS):
        q = jnp.concatenate(
            [q_ref[:, (g * r + i) * HEAD_DIM:(g * r + i + 1) * HEAD_DIM] for i in range(r)], axis=0)
        s = lax.dot_general(q, kc_ref[0, 0, g], (((1,), (1,)), ((), ())), preferred_element_type=F32)
        s = jnp.where(cmp_ok, s, MASKED)
        m = jnp.max(s, axis=-1, keepdims=True)
        p = jnp.where(cmp_ok, jnp.exp(s - m), 0.0)
        den = jnp.sum(p, axis=-1, keepdims=True)
        p = p / jnp.where(den > 0, den, 1.0)
        o = jnp.dot(p.astype(BF16), vc_ref[0, 0, g], preferred_element_type=F32)
        for i in range(r):
            oc_ref[:, (g * r + i) * HEAD_DIM:(g * r + i + 1) * HEAD_DIM] = o[i * tq:(i + 1) * tq].astype(oc_ref.dtype)
        psum = p[0:tq]
        for i in range(1, r):
            psum = psum + p[i * tq:(i + 1) * tq]
        imp = sum(jnp.dot(piece, map_ref[...], preferred_element_type=F32) for piece in _split3(psum))
        score = jnp.where(valid, imp + jnp.where(forced, FORCE_BONUS, 0.0), NEG_SCORE)
        st = score.T[0:n_sel]
        rank = jnp.zeros((n_sel, tq), F32)
        for i in range(n_sel):
            ri = st[i:i + 1]
            beats = jnp.where(ri > st, 1.0, jnp.where((ri == st) & (jsub > i), 1.0, 0.0))
            rank = rank + beats
        sel_rows.append(jnp.where(rank < top_n, 1.0, 0.0))
    pad = LANES - NSA_KV_HEADS * n_sel
    if pad:
        sel_rows.append(jnp.zeros((pad, tq), F32))
    sel_ref[...] = jnp.concatenate(sel_rows, axis=0).T.astype(sel_ref.dtype)


def _sel_map(ncmp_pad, n_sel):
    cst = np.arange(ncmp_pad)[:, None] * CMP_STRIDE
    sst = np.arange(LANES)[None, :] * SEL_BLOCK
    ov = np.clip(np.minimum(cst + CMP_BLOCK, sst + SEL_BLOCK) - np.maximum(cst, sst), 0, None) / CMP_BLOCK
    ov[:, n_sel:] = 0.0
    ov[ncmp_pad - 1:, :] = 0.0
    return jnp.asarray(ov, BF16)


def _cmp_attn(qn, kvc, bsz, t):
    n, qw = qn.shape
    ncmp = kvc.shape[3]
    n_sel = t // SEL_BLOCK
    top_n = min(SEL_TOPN, n_sel)
    assert NSA_KV_HEADS * n_sel <= LANES and ncmp <= LANES
    tq = _pick(t, 256)
    nq = t // tq
    kv_spec = lambda kv: pl.BlockSpec((1, 1, NSA_KV_HEADS, ncmp, HEAD_DIM), lambda b, i: (kv, b, 0, 0, 0))
    return pl.pallas_call(
        functools.partial(_cmp_attn_kernel, n_sel=n_sel, top_n=top_n),
        grid=(bsz, nq),
        in_specs=[pl.BlockSpec((tq, qw), lambda b, i: (b * nq + i, 0)),
                  kv_spec(0), kv_spec(1),
                  pl.BlockSpec((ncmp, LANES), lambda b, i: (0, 0))],
        out_specs=[pl.BlockSpec((tq, qw), lambda b, i: (b * nq + i, 0)),
                   pl.BlockSpec((tq, LANES), lambda b, i: (b * nq + i, 0))],
        out_shape=[jax.ShapeDtypeStruct((n, qw), BF16), jax.ShapeDtypeStruct((n, LANES), BF16)],
        compiler_params=_params("parallel", "parallel"),
        name="cmp_attn",
    )(qn, kvc, kvc, _sel_map(ncmp, n_sel))


def _flash_branch(q, k_ref, v_ref, lo, hi, tk, mask_fn):
    m_rows = q.shape[0]

    def body(j, carry):
        m, l, acc = carry
        start = pl.multiple_of(j * tk, tk)
        k = k_ref[pl.ds(start, tk), :]
        v = v_ref[pl.ds(start, tk), :]
        s = lax.dot_general(q, k, (((1,), (1,)), ((), ())), preferred_element_type=F32)
        ok = mask_fn(j)
        s = jnp.where(ok, s, MASKED)
        m_new = jnp.maximum(m, jnp.max(s, axis=-1, keepdims=True))
        alpha = jnp.exp(m - m_new)
        p = jnp.where(ok, jnp.exp(s - m_new), 0.0)
        l = alpha * l + jnp.sum(p, axis=-1, keepdims=True)
        acc = alpha * acc + jnp.dot(p.astype(BF16), v, preferred_element_type=F32)
        return m_new, l, acc

    init = (jnp.full((m_rows, 1), MASKED, F32), jnp.zeros((m_rows, 1), F32), jnp.zeros((m_rows, HEAD_DIM), F32))
    _, l, acc = lax.fori_loop(lo, hi, body, init)
    return acc / jnp.where(l > 0, l, 1.0)


def _sw_attn_kernel(q_ref, ks_ref, vs_ref, kw_ref, vw_ref, sel_ref, oc_ref, gate_ref, o_ref, *, tk, n_sel):
    tq = q_ref.shape[0]
    r = NSA_GROUP
    g = pl.program_id(1)
    t0 = pl.program_id(2) * tq
    q = jnp.concatenate([q_ref[:, i * HEAD_DIM:(i + 1) * HEAD_DIM] for i in range(r)], axis=0)
    tcol = t0 + lax.broadcasted_iota(jnp.int32, (tq, 1), 0)
    tcol_r = jnp.concatenate([tcol] * r, axis=0)
    sel = sel_ref[...]
    lane = lax.broadcasted_iota(jnp.int32, (1, tk), 1)
    crow = lax.broadcasted_iota(jnp.int32, (LANES, 1), 0)

    def sel_mask(j):
        kpos = j * tk + lane
        expand = jnp.where(crow == g * n_sel + kpos // SEL_BLOCK, 1.0, 0.0).astype(BF16)
        chosen = jnp.dot(sel, expand, preferred_element_type=F32)
        chosen = jnp.concatenate([chosen] * r, axis=0)
        return jnp.where(kpos <= tcol_r, chosen, 0.0) > 0.5

    def win_mask(j):
        dist = tcol_r - (j * tk + lane)
        return (dist >= 0) & (dist < WINDOW)

    hi = (t0 + tq + tk - 1) // tk
    o_s = _flash_branch(q, ks_ref, vs_ref, 0, hi, tk, sel_mask)
    lo_w = jnp.maximum(t0 - (WINDOW - 1), 0) // tk
    o_w = _flash_branch(q, kw_ref, vw_ref, lo_w, hi, tk, win_mask)
    gate = jax.nn.sigmoid(gate_ref[...])
    for i in range(r):
        rows = slice(i * tq, (i + 1) * tq)
        cols = slice(i * HEAD_DIM, (i + 1) * HEAD_DIM)
        o = (gate[:, 3 * i:3 * i + 1] * oc_ref[:, cols].astype(F32)
             + gate[:, 3 * i + 1:3 * i + 2] * o_s[rows]
             + gate[:, 3 * i + 2:3 * i + 3] * o_w[rows])
        o_ref[:, cols] = o.astype(o_ref.dtype)


def _sw_attn(qr, kr, proj, sel, o_c, gates, bsz, t, vsel_col, vwin_col):
    n, qw = qr.shape
    g = NSA_KV_HEADS
    gw = NSA_GROUP * HEAD_DIM
    tq = _pick(t, 256)
    tk = tq
    nq = t // tq
    n_sel = t // SEL_BLOCK
    vs0, vw0 = vsel_col // HEAD_DIM, vwin_col // HEAD_DIM
    qspec = pl.BlockSpec((tq, gw), lambda b, gi, i: (b * nq + i, gi))
    return pl.pallas_call(
        functools.partial(_sw_attn_kernel, tk=tk, n_sel=n_sel),
        grid=(bsz, g, nq),
        in_specs=[qspec,
                  pl.BlockSpec((t, HEAD_DIM), lambda b, gi, i: (b, gi)),
                  pl.BlockSpec((t, HEAD_DIM), lambda b, gi, i: (b, vs0 + gi)),
                  pl.BlockSpec((t, HEAD_DIM), lambda b, gi, i: (b, g + gi)),
                  pl.BlockSpec((t, HEAD_DIM), lambda b, gi, i: (b, vw0 + gi)),
                  pl.BlockSpec((tq, LANES), lambda b, gi, i: (b * nq + i, 0)),
                  qspec,
                  pl.BlockSpec((tq, LANES), lambda b, gi, i: (b * nq + i, gi))],
        out_specs=qspec,
        out_shape=jax.ShapeDtypeStruct((n, qw), BF16),
        compiler_params=_params("parallel", "parallel", "parallel"),
        name="sw_attn",
    )(qr, kr, proj, kr, proj, sel, o_c, gates)


def _outproj_kernel(a1_ref, a2_ref, w1_ref, w2_ref, x_ref, g_ref, o_ref):
    y = jnp.dot(a1_ref[...], w1_ref[...], preferred_element_type=F32)
    y = y + jnp.dot(a2_ref[...], w2_ref[...], preferred_element_type=F32)
    o_ref[...] = x_ref[...] + g_ref[0] * y


def _outproj(y_pool, y_nsa, w_out, x2d, gate, bsz, t):
    n, d = x2d.shape
    k1, k2 = y_pool.shape[1], y_nsa.shape[1]
    tm = _pick(t, 1024)
    tn = _pick(d, 1024)
    nt = t // tm
    w = w_out.astype(BF16)
    return pl.pallas_call(
        _outproj_kernel,
        grid=(n // tm, d // tn),
        in_specs=[pl.BlockSpec((tm, k1), lambda i, j: (i, 0)),
                  pl.BlockSpec((tm, k2), lambda i, j: (i, 0)),
                  pl.BlockSpec((k1, tn), lambda i, j: (0, j)),
                  pl.BlockSpec((k2, tn), lambda i, j: (0, j)),
                  pl.BlockSpec((tm, tn), lambda i, j: (i, j)),
                  pl.BlockSpec((1, 1, tn), lambda i, j: (i // nt, 0, j))],
        out_specs=pl.BlockSpec((tm, tn), lambda i, j: (i, j)),
        out_shape=jax.ShapeDtypeStruct((n, d), F32),
        compiler_params=_params("parallel", "parallel"),
        name="outproj",
    )(y_pool, y_nsa, w[:k1], w[k1:], x2d, gate.reshape(bsz, 1, d))


def _cmpx(v, i, j):
    hi = jnp.maximum(v[i], v[j])
    lo = jnp.minimum(v[i], v[j])
    v[i], v[j] = hi, lo


def _bitonic_merge_desc(v):
    n = len(v)
    d = n // 2
    while d >= 1:
        for i in range(n):
            if (i // d) % 2 == 0:
                _cmpx(v, i, i + d)
        d //= 2


def _sort_desc(v):
    n = len(v)
    size = 2
    while size <= n:
        for base in range(0, n, size):
            blk = v[base:base + size]
            blk[size // 2:] = blk[size // 2:][::-1]
            _bitonic_merge_desc(blk)
            v[base:base + size] = blk
        size *= 2


def _merge_top(a, b):
    n = len(a)
    v = [jnp.maximum(a[i], b[n - 1 - i]) for i in range(n)]
    _bitonic_merge_desc(v)
    return v


def _top_sublanes(s, k):
    nk = s.shape[0]
    groups = nk // 8
    assert groups == k
    v = [s[8 * i:8 * i + 8] for i in range(groups)]
    _sort_desc(v)
    shift = 4
    while shift >= 1:
        v = _merge_top(v, [pltpu.roll(x, shift, axis=0) for x in v])
        shift //= 2
    return v


def _route_kernel(qt_ref, keys_ref, st_ref, stat_ref):
    k = PEER_TOPK
    nh = keys_ref.shape[0]
    kd = keys_ref.shape[3]
    neg = jnp.full((8, qt_ref.shape[1]), -jnp.inf, F32)
    thr_rows, m0_rows, m1_rows, iz_rows = [], [], [], []
    for h in range(nh):
        tops = []
        for c in range(2):
            row0 = (h * 2 + c) * kd
            qa, qb, qc = _split3(qt_ref[row0:row0 + kd, :])
            ka, kb, _ = _split3(keys_ref[h, c])
            s = (jnp.dot(ka, qa, preferred_element_type=F32) + jnp.dot(ka, qb, preferred_element_type=F32)
                 + jnp.dot(kb, qa, preferred_element_type=F32) + jnp.dot(ka, qc, preferred_element_type=F32)
                 + jnp.dot(kb, qb, preferred_element_type=F32))
            st_ref[(h * 2 + c) * PEER_NKEYS:(h * 2 + c + 1) * PEER_NKEYS, :] = s
            tops.append(_top_sublanes(s, k))
        v0, v1 = tops
        cands = [v0[a] + v1[b] for a in range(k) for b in range(k) if (a + 1) * (b + 1) <= k]
        lists = []
        for base in range(0, len(cands), k):
            blk = cands[base:base + k]
            blk = blk + [neg] * (k - len(blk))
            _sort_desc(blk)
            lists.append(blk)
        best = lists[0]
        for other in lists[1:]:
            best = _merge_top(best, other)
        z = sum(jnp.exp(b - best[0]) for b in best)
        thr_rows.append(best[k - 1][0:1])
        m0_rows.append(v0[0][0:1])
        m1_rows.append(v1[0][0:1])
        iz_rows.append((1.0 / z)[0:1])
    stat_ref[...] = jnp.concatenate(thr_rows + m0_rows + m1_rows + iz_rows, axis=0)


def _route(qt, peer_keys):
    rows, n = qt.shape
    nh, _, nk, kd = peer_keys.shape
    assert nk == PEER_NKEYS and nk == 8 * PEER_TOPK
    tr = _pick(n, 512)
    return pl.pallas_call(
        _route_kernel,
        grid=(n // tr,),
        in_specs=[pl.BlockSpec((rows, tr), lambda i: (0, i)),
                  pl.BlockSpec((nh, 2, nk, kd), lambda i: (0, 0, 0, 0))],
        out_specs=[pl.BlockSpec((nh * 2 * nk, tr), lambda i: (0, i)),
                   pl.BlockSpec((4 * nh, tr), lambda i: (0, i))],
        out_shape=[jax.ShapeDtypeStruct((nh * 2 * nk, n), F32), jax.ShapeDtypeStruct((4 * nh, n), F32)],
        compiler_params=_params("parallel"),
        name="route",
    )(qt, peer_keys)


def _peer_kernel(ht_ref, u_ref, v_ref, st_ref, stat_ref, o_ref, e0_ref, e1_ref, wa_ref, new_ref, pre_ref,
                 pre_new_ref, rows_ref, *, n_tiles, n_pairs):
    s = pl.program_id(0)
    te = u_ref.shape[0]
    nk = PEER_NKEYS
    nh = e0_ref.shape[0]
    per_step = te // nk
    pair1 = jnp.clip(s - 1, 0, n_pairs - 1)

    @pl.when(s == 0)
    def _():
        o_ref[...] = jnp.zeros_like(o_ref)
        wa_ref[...] = jnp.zeros_like(wa_ref)
        pre_ref[...] = jnp.zeros_like(pre_ref)

    @pl.when((s == 0) | ((s >= 1) & ((s - 1) % n_tiles == 0)))
    def _():
        for h in range(nh):
            m0 = stat_ref[nh + h:nh + h + 1, :]
            m1 = stat_ref[2 * nh + h:2 * nh + h + 1, :]
            iz = stat_ref[3 * nh + h:3 * nh + h + 1, :]
            e0_ref[h] = jnp.exp(st_ref[(2 * h) * nk:(2 * h + 1) * nk, :] - m0)
            e1_ref[h] = jnp.exp(st_ref[(2 * h + 1) * nk:(2 * h + 2) * nk, :] - m1) * iz

    @pl.when((s >= 2) & ((s - 2) % n_tiles == 0))
    def _():
        o_ref[...] = jnp.zeros_like(o_ref)

    tm = ht_ref.shape[1]
    d = v_ref.shape[1]
    tile = pair1 % n_tiles
    n_split = 2
    th = tm // n_split
    dq = d // 4

    def first_matmul(half, row0):
        cols = slice(half * th, (half + 1) * th)
        pre_new_ref[:, cols] = jnp.dot(u_ref[pl.ds(row0, te), :], ht_ref[:, cols], preferred_element_type=F32)

    def second_matmul(q, row0):
        cols = slice(q * dq, (q + 1) * dq)
        o_ref[:, cols] += jnp.dot(wa_ref[pl.ds(row0, tm), :], v_ref[:, cols], preferred_element_type=F32)

    for a in range(per_step):
        i0 = tile * per_step + a
        for h in range(nh):
            k = (a * nh + h) * 2
            rows_ref[k:k + 1, :] = st_ref[pl.ds((2 * h) * nk + i0, 1), :]
            rows_ref[k + 1:k + 2, :] = e0_ref[h, pl.ds(i0, 1), :]

    def gate_piece(a, half):
        cols = slice(half * th, (half + 1) * th)
        w = None
        for h in range(nh):
            k = (a * nh + h) * 2
            s0 = rows_ref[k:k + 1, cols]
            e0 = rows_ref[k + 1:k + 2, cols]
            ssum = s0 + st_ref[(2 * h + 1) * nk:(2 * h + 2) * nk, cols]
            wh = jnp.where(ssum >= stat_ref[h:h + 1, cols], e0 * e1_ref[h, :, cols], 0.0)
            w = wh if w is None else w + wh
        prod = w * _gelu(pre_ref[a * nk:(a + 1) * nk, cols])
        new_ref[cols, a * nk:(a + 1) * nk] = prod.T.astype(BF16)
        folded = jnp.max(prod.reshape(nk // 8, 8, th), axis=0)
        return functools.reduce(jnp.maximum, [folded[:, c * LANES:(c + 1) * LANES] for c in range(th // LANES)])

    def zero_after(vregs):
        bits = pltpu.bitcast(functools.reduce(jnp.maximum, vregs), jnp.uint32)
        zero = lax.shift_right_logical(lax.shift_right_logical(bits, jnp.uint32(16)), jnp.uint32(16))
        return pl.multiple_of(zero[0, 0].astype(jnp.int32), 16)

    pieces = [(a, half) for a in range(per_step) for half in range(n_split)]
    per_chunk = -(-len(pieces) // 4)
    chunks = [functools.partial(first_matmul, 1)] + [functools.partial(second_matmul, q) for q in range(3)]
    first_matmul(0, 0)
    for ci, chunk in enumerate(chunks):
        chunk(zero_after([gate_piece(a, half) for a, half in pieces[ci * per_chunk:(ci + 1) * per_chunk]]))
    second_matmul(3, 0)
    wa_ref[...] = new_ref[...]
    pre_ref[...] = pre_new_ref[...]


def _peer(ht, u, v, st, stats):
    d, n = ht.shape
    e = u.shape[0]
    nh = stats.shape[0] // 4
    tm = _pick(n, 512)
    te = _pick(e, 512)
    n_tiles = e // te
    n_pairs = (n // tm) * n_tiles
    pair = lambda s, lag: jnp.clip(s - lag, 0, n_pairs - 1)
    return pl.pallas_call(
        functools.partial(_peer_kernel, n_tiles=n_tiles, n_pairs=n_pairs),
        grid=(n_pairs + 2,),
        in_specs=[pl.BlockSpec((d, tm), lambda s: (0, pair(s, 0) // n_tiles)),
                  pl.BlockSpec((te, d), lambda s: (pair(s, 0) % n_tiles, 0)),
                  pl.BlockSpec((te, d), lambda s: (pair(s, 2) % n_tiles, 0)),
                  pl.BlockSpec((st.shape[0], tm), lambda s: (0, pair(s, 1) // n_tiles)),
                  pl.BlockSpec((stats.shape[0], tm), lambda s: (0, pair(s, 1) // n_tiles))],
        out_specs=pl.BlockSpec((tm, d), lambda s: (pair(s, 2) // n_tiles, 0)),
        out_shape=jax.ShapeDtypeStruct((n, d), F32),
        scratch_shapes=[pltpu.VMEM((nh, PEER_NKEYS, tm), F32), pltpu.VMEM((nh, PEER_NKEYS, tm), F32),
                        pltpu.VMEM((tm, te), BF16), pltpu.VMEM((tm, te), BF16),
                        pltpu.VMEM((te, tm), F32), pltpu.VMEM((te, tm), F32),
                        pltpu.VMEM((2 * nh * (te // PEER_NKEYS), tm), F32)],
        compiler_params=_params("arbitrary"),
        name="peer",
    )(ht, u, v, st, stats)


def _combine_kernel(x_ref, y_ref, g_ref, o_ref):
    o_ref[0] = x_ref[0] + g_ref[0] * y_ref[0]


def _combine(x, y, gate):
    bsz, t, d = x.shape
    tt = _pick(t, 256)
    spec = pl.BlockSpec((1, tt, d), lambda b, i: (b, i, 0))
    return pl.pallas_call(
        _combine_kernel,
        grid=(bsz, t // tt),
        in_specs=[spec, spec, pl.BlockSpec((1, 1, d), lambda b, i: (b, 0, 0))],
        out_specs=spec,
        out_shape=jax.ShapeDtypeStruct((bsz, t, d), F32),
        compiler_params=_params("parallel", "parallel"),
        name="combine",
    )(x, y, gate.reshape(bsz, 1, d))


def _layer(x, c, w_ada, b_ada, norm1_g, norm2_g, w_in, w_out, w_pool, pool_scale, q_norm_g, k_norm_g,
           cmp_pe, cmp_w1, cmp_w2, w_pq, peer_keys, peer_u, peer_v):
    bsz, t, d = x.shape
    n = bsz * t
    g, r = NSA_KV_HEADS, NSA_GROUP
    pool_w = w_pool.shape[0] * w_pool.shape[1]
    nsa_w = g * r * HEAD_DIM
    kv_w = N_BRANCH * 2 * g * HEAD_DIM
    gate_w = g * r * N_BRANCH
    assert w_in.shape[1] == pool_w + nsa_w + kv_w + gate_w
    q_col, kv_col = pool_w, pool_w + nsa_w
    branch_w = g * HEAD_DIM
    assert pool_w % nsa_w in (0, pool_w) and q_col % 1024 == 0

    mod = _ada(c, w_ada, b_ada)
    shift1, scale1, gate1, shift2, scale2, gate2 = [mod[:, i * d:(i + 1) * d] for i in range(N_MOD)]

    h = _modnorm(x, norm1_g, scale1, shift1, transpose=False).reshape(n, d)
    main_w = pool_w + nsa_w + kv_w
    w_main = w_in[:, :main_w].astype(BF16)
    w_gate = w_in[:, main_w:].reshape(d, g, r * N_BRANCH)
    w_gate = jnp.pad(w_gate, ((0, 0), (0, 0), (0, LANES - r * N_BRANCH))).reshape(d, g * LANES).astype(BF16)
    proj = _matmul(h, w_main, BF16, "inproj")
    gates = _matmul(h, w_gate, F32, "gateproj")

    y_pool = _pool(proj, w_pool, pool_scale, bsz, t)

    scale = HEAD_DIM ** -0.5
    qblk = 1024
    qn, qr = _headnorm(proj, q_norm_g.reshape(1, 1, HEAD_DIM), t, qblk, lambda j: q_col // qblk + j, nsa_w // qblk,
                       lambda j: 0, scale, True, "qprep")
    ksel_blk = (kv_col + 2 * branch_w) // branch_w
    kr = _headnorm(proj, k_norm_g.reshape(N_BRANCH, 1, HEAD_DIM), t, branch_w, lambda j: ksel_blk + 2 * j, 2,
                   lambda j: 1 + j, 1.0, False, "kprep")
    kvc = _compress(proj, cmp_pe, cmp_w1, cmp_w2, k_norm_g[0], bsz, t, kv_col)
    o_c, sel = _cmp_attn(qn, kvc, bsz, t)
    y_nsa = _sw_attn(qr, kr, proj, sel, o_c, gates, bsz, t, kv_col + 3 * branch_w, kv_col + 5 * branch_w)

    x1 = _outproj(y_pool, y_nsa, w_out, x.reshape(n, d), gate1, bsz, t)

    ht = _modnorm(x1.reshape(bsz, t, d), norm2_g, scale2, shift2, transpose=True)
    qt = _matmul(w_pq.T.astype(BF16), ht, F32, "peer_q")
    st, stats = _route(qt, peer_keys)
    y = _peer(ht, peer_u.astype(BF16), peer_v.astype(BF16), st, stats)
    return _combine(x1.reshape(bsz, t, d), y.reshape(bsz, t, d), gate2)


def kernel(x, c, w_ada, b_ada, norm1_g, norm2_g, w_in, w_out, w_pool, pool_scale, q_norm_g, k_norm_g, cmp_pe,
           cmp_w1, cmp_w2, w_pq, peer_keys, peer_u, peer_v):
    for l in range(w_ada.shape[0]):
        x = _layer(x, c, w_ada[l], b_ada[l], norm1_g[l], norm2_g[l], w_in[l], w_out[l], w_pool[l], pool_scale[l],
                   q_norm_g[l], k_norm_g[l], cmp_pe[l], cmp_w1[l], cmp_w2[l], w_pq[l], peer_keys[l], peer_u[l],
                   peer_v[l])
    return x
```

```python
import functools
import math

import numpy as np
import jax
import jax.numpy as jnp
from jax import lax
from jax.experimental import pallas as pl
from jax.experimental.pallas import tpu as pltpu

F32 = jnp.float32
BF16 = jnp.bfloat16

HEAD_DIM = 128
POOL_WINDOWS = (2, 4, 8, 16)
NSA_GROUP = 6
NSA_KV_HEADS = 4
N_BRANCH = 3
CMP_BLOCK = 32
CMP_STRIDE = 16
SEL_BLOCK = 64
SEL_TOPN = 16
WINDOW = 512
FORCE_BONUS = 1000.0
NEG_SCORE = -1e9
ROPE_THETA = 10000.0
PEER_HEADS = 8
PEER_NKEYS = 128
PEER_TOPK = 16
NORM_EPS = 1e-6
N_MOD = 6

V7X_VMEM_BYTES = 64 * 1024 * 1024
VMEM_LIMIT = V7X_VMEM_BYTES - 6 * 1024 * 1024
LANES = 128
MASKED = -1e30


def _params(*sem):
    return pltpu.CompilerParams(dimension_semantics=sem, vmem_limit_bytes=VMEM_LIMIT)


def _pick(n, pref):
    t = min(n, pref)
    while n % t:
        t //= 2
    return t


def _gelu(x):
    return 0.5 * x * (1.0 + lax.erf(x * (1.0 / math.sqrt(2.0))))


def _ada_kernel(c_ref, w_ref, b_ref, o_ref):
    c = c_ref[...]
    a = (c * jax.nn.sigmoid(c)).astype(BF16)
    o_ref[...] = jnp.dot(a, w_ref[...].astype(BF16), preferred_element_type=F32) + b_ref[...]


def _ada(c, w_ada, b_ada):
    bsz, d = c.shape
    n = w_ada.shape[1]
    tn = _pick(n, 512)
    return pl.pallas_call(
        _ada_kernel,
        grid=(n // tn,),
        in_specs=[pl.BlockSpec((bsz, d), lambda j: (0, 0)),
                  pl.BlockSpec((d, tn), lambda j: (0, j)),
                  pl.BlockSpec((1, tn), lambda j: (0, j))],
        out_specs=pl.BlockSpec((bsz, tn), lambda j: (0, j)),
        out_shape=jax.ShapeDtypeStruct((bsz, n), F32),
        compiler_params=_params("parallel"),
        name="ada",
    )(c, w_ada, b_ada.reshape(1, n))


def _modnorm_kernel(x_ref, g_ref, sc_ref, sh_ref, o_ref, *, transpose):
    x = x_ref[0]
    ms = jnp.mean(x * x, axis=-1, keepdims=True)
    y = x * lax.rsqrt(ms + NORM_EPS) * g_ref[...]
    y = y * (1.0 + sc_ref[0]) + sh_ref[0]
    if transpose:
        o_ref[...] = y.T.astype(BF16)
    else:
        o_ref[0] = y.astype(BF16)


def _modnorm(x, g, scale, shift, transpose):
    bsz, t, d = x.shape
    tt = _pick(t, 256)
    nt = t // tt
    if transpose:
        out_spec = pl.BlockSpec((d, tt), lambda b, i: (0, b * nt + i))
        out_shape = jax.ShapeDtypeStruct((d, bsz * t), BF16)
    else:
        out_spec = pl.BlockSpec((1, tt, d), lambda b, i: (b, i, 0))
        out_shape = jax.ShapeDtypeStruct((bsz, t, d), BF16)
    return pl.pallas_call(
        functools.partial(_modnorm_kernel, transpose=transpose),
        grid=(bsz, nt),
        in_specs=[pl.BlockSpec((1, tt, d), lambda b, i: (b, i, 0)),
                  pl.BlockSpec((1, d), lambda b, i: (0, 0)),
                  pl.BlockSpec((1, 1, d), lambda b, i: (b, 0, 0)),
                  pl.BlockSpec((1, 1, d), lambda b, i: (b, 0, 0))],
        out_specs=out_spec,
        out_shape=out_shape,
        compiler_params=_params("parallel", "parallel"),
        name="modnorm_t" if transpose else "modnorm",
    )(x, g.reshape(1, d), scale.reshape(bsz, 1, d), shift.reshape(bsz, 1, d))


def _mm_kernel(a_ref, w_ref, o_ref):
    o_ref[...] = jnp.dot(a_ref[...], w_ref[...], preferred_element_type=F32).astype(o_ref.dtype)


def _matmul(a, w, out_dtype, name):
    m, k = a.shape
    n = w.shape[1]
    tm = _pick(m, 1024)
    tn = _pick(n, 1024)
    return pl.pallas_call(
        _mm_kernel,
        grid=(m // tm, n // tn),
        in_specs=[pl.BlockSpec((tm, k), lambda i, j: (i, 0)),
                  pl.BlockSpec((k, tn), lambda i, j: (0, j))],
        out_specs=pl.BlockSpec((tm, tn), lambda i, j: (i, j)),
        out_shape=jax.ShapeDtypeStruct((m, n), out_dtype),
        compiler_params=_params("parallel", "parallel"),
        name=name,
    )(a, w)


def _pool_kernel(p_ref, w_ref, s_ref, o_ref):
    t = p_ref.shape[0]
    gw = w_ref.shape[1]
    row = lax.broadcasted_iota(jnp.int32, (t, 1), 0)

    def shifted(v, k):
        return jnp.where(row >= k, pltpu.roll(v, k, axis=0), 0.0)

    for gi, win in enumerate(POOL_WINDOWS):
        p = p_ref[:, gi * gw:(gi + 1) * gw].astype(F32)
        acc = p
        span = 1
        while span < win:
            acc = acc + shifted(acc, span)
            span *= 2
        count = jnp.minimum(row + 1, win).astype(F32)
        dlt = acc / count - p
        y = jnp.dot(dlt.astype(BF16), w_ref[gi], preferred_element_type=F32)
        o_ref[:, gi * gw:(gi + 1) * gw] = (y * s_ref[:, gi * gw:(gi + 1) * gw]).astype(o_ref.dtype)


def _pool(proj, w_pool, pool_scale, bsz, t):
    ng, gw, _ = w_pool.shape
    pw = ng * gw
    return pl.pallas_call(
        _pool_kernel,
        grid=(bsz,),
        in_specs=[pl.BlockSpec((t, pw), lambda b: (b, 0)),
                  pl.BlockSpec((ng, gw, gw), lambda b: (0, 0, 0)),
                  pl.BlockSpec((1, pw), lambda b: (0, 0))],
        out_specs=pl.BlockSpec((t, pw), lambda b: (b, 0)),
        out_shape=jax.ShapeDtypeStruct((bsz * t, pw), BF16),
        compiler_params=_params("parallel"),
        name="pool",
    )(proj, w_pool.astype(BF16), pool_scale.reshape(1, pw))


def _headnorm_kernel(x_ref, g_ref, cos_ref, sin_ref, *o_refs, n_heads, out_scale, with_plain):
    cosf = cos_ref[...]
    sinf = sin_ref[...]
    g = g_ref[0]
    for h in range(n_heads):
        sl = slice(h * HEAD_DIM, (h + 1) * HEAD_DIM)
        x = x_ref[:, sl].astype(F32)
        ms = jnp.mean(x * x, axis=-1, keepdims=True)
        y = x * lax.rsqrt(ms + NORM_EPS) * g
        rot = y * cosf + pltpu.roll(y, HEAD_DIM // 2, axis=1) * sinf
        if with_plain:
            o_refs[0][:, sl] = (y * out_scale).astype(BF16)
            o_refs[1][:, sl] = (rot * out_scale).astype(BF16)
        else:
            o_refs[0][:, sl] = (rot * out_scale).astype(BF16)


def _rope_tables(t):
    half = HEAD_DIM // 2
    inv = ROPE_THETA ** (-np.arange(half, dtype=np.float64) / half)
    ang = np.arange(t, dtype=np.float64)[:, None] * inv[None, :]
    cosf = np.concatenate([np.cos(ang), np.cos(ang)], axis=-1)
    sinf = np.concatenate([-np.sin(ang), np.sin(ang)], axis=-1)
    return jnp.asarray(cosf, F32), jnp.asarray(sinf, F32)


def _headnorm(proj, gamma, t, width, col_of, n_col_blocks, gamma_of, out_scale, with_plain, name):
    n = proj.shape[0]
    tt = _pick(t, 512)
    nt = t // tt
    cosf, sinf = _rope_tables(t)
    n_out = 2 if with_plain else 1
    out_w = n_col_blocks * width
    out = pl.pallas_call(
        functools.partial(_headnorm_kernel, n_heads=width // HEAD_DIM, out_scale=out_scale, with_plain=with_plain),
        grid=(n // tt, n_col_blocks),
        in_specs=[pl.BlockSpec((tt, width), lambda i, j: (i, col_of(j))),
                  pl.BlockSpec((1, 1, HEAD_DIM), lambda i, j: (gamma_of(j), 0, 0)),
                  pl.BlockSpec((tt, HEAD_DIM), lambda i, j: (i % nt, 0)),
                  pl.BlockSpec((tt, HEAD_DIM), lambda i, j: (i % nt, 0))],
        out_specs=[pl.BlockSpec((tt, width), lambda i, j: (i, j))] * n_out,
        out_shape=[jax.ShapeDtypeStruct((n, out_w), BF16)] * n_out,
        compiler_params=_params("parallel", "parallel"),
        name=name,
    )(proj, gamma, cosf, sinf)
    return out if with_plain else out[0]


def _compress_kernel(x_ref, pe_ref, w1_ref, w2_ref, g_ref, o_ref, xs_ref):
    t = x_ref.shape[0]
    nblk = t // CMP_STRIDE
    kv = pl.program_id(1)
    xs_ref[...] = x_ref[...].astype(F32)
    cols = [xs_ref[pl.ds(l, nblk, stride=CMP_STRIDE), :] for l in range(CMP_STRIDE)]
    xc = jnp.concatenate(cols, axis=1)
    lo = jnp.dot((xc + pe_ref[0, 0]).astype(BF16), w1_ref[0, 0], preferred_element_type=F32)
    hi = jnp.dot((xc + pe_ref[0, 1]).astype(BF16), w1_ref[0, 1], preferred_element_type=F32)
    pre = lo + pltpu.roll(hi, nblk - 1, axis=0)
    hid = _gelu(pre)
    out = jnp.dot(hid.astype(BF16), w2_ref[0], preferred_element_type=F32)
    ms = jnp.mean(out * out, axis=-1, keepdims=True)
    normed = out * lax.rsqrt(ms + NORM_EPS) * g_ref[...]
    out = jnp.where(kv == 0, normed, out)
    row = lax.broadcasted_iota(jnp.int32, out.shape, 0)
    o_ref[0, 0, 0] = jnp.where(row < nblk - 1, out, 0.0).astype(BF16)


def _compress(proj, cmp_pe, cmp_w1, cmp_w2, k_gamma, bsz, t, col0):
    g = NSA_KV_HEADS
    nblk = t // CMP_STRIDE
    hidden = cmp_w1.shape[-1]
    halves = CMP_BLOCK // CMP_STRIDE
    w1 = cmp_w1.astype(BF16).reshape(2, halves, CMP_STRIDE * HEAD_DIM, hidden)
    pe = cmp_pe.reshape(2, halves, 1, CMP_STRIDE * HEAD_DIM)
    cb0 = col0 // HEAD_DIM
    return pl.pallas_call(
        _compress_kernel,
        grid=(bsz, 2, g),
        in_specs=[pl.BlockSpec((t, HEAD_DIM), lambda b, kv, gi: (b, cb0 + kv * g + gi)),
                  pl.BlockSpec((1, halves, 1, CMP_STRIDE * HEAD_DIM), lambda b, kv, gi: (kv, 0, 0, 0)),
                  pl.BlockSpec((1, halves, CMP_STRIDE * HEAD_DIM, hidden), lambda b, kv, gi: (kv, 0, 0, 0)),
                  pl.BlockSpec((1, hidden, HEAD_DIM), lambda b, kv, gi: (kv, 0, 0)),
                  pl.BlockSpec((1, HEAD_DIM), lambda b, kv, gi: (0, 0))],
        out_specs=pl.BlockSpec((1, 1, 1, nblk, HEAD_DIM), lambda b, kv, gi: (kv, b, gi, 0, 0)),
        out_shape=jax.ShapeDtypeStruct((2, bsz, g, nblk, HEAD_DIM), BF16),
        scratch_shapes=[pltpu.VMEM((t, HEAD_DIM), F32)],
        compiler_params=_params("parallel", "parallel", "parallel"),
        name="compress",
    )(proj, pe, w1, cmp_w2.astype(BF16), k_gamma.reshape(1, HEAD_DIM))


def _split3(x):
    a = x.astype(BF16)
    r = x - a.astype(F32)
    b = r.astype(BF16)
    c = (r - b.astype(F32)).astype(BF16)
    return a, b, c


def _cmp_attn_kernel(q_ref, kc_ref, vc_ref, map_ref, oc_ref, sel_ref, *, n_sel, top_n):
    tq = q_ref.shape[0]
    ncmp = kc_ref.shape[3]
    r = NSA_GROUP
    t0 = pl.program_id(1) * tq
    tcol = t0 + lax.broadcasted_iota(jnp.int32, (tq, 1), 0)
    tcol_r = jnp.concatenate([tcol] * r, axis=0)
    nrow = lax.broadcasted_iota(jnp.int32, (1, ncmp), 1)
    cmp_ok = (nrow * CMP_STRIDE + (CMP_BLOCK - 1) <= tcol_r) & (nrow < ncmp - 1)
    jrow = lax.broadcasted_iota(jnp.int32, (1, ncmp), 1)
    cur = tcol // SEL_BLOCK
    forced = (jrow == 0) | (jrow == cur) | (jrow == cur - 1)
    valid = jrow * SEL_BLOCK <= tcol
    jsub = lax.broadcasted_iota(jnp.int32, (n_sel, tq), 0)
    sel_rows = []
    for g in range(NSA_KV_HEADS):
        q = jnp.concatenate(
            [q_ref[:, (g * r + i) * HEAD_DIM:(g * r + i + 1) * HEAD_DIM] for i in range(r)], axis=0)
        s = lax.dot_general(q, kc_ref[0, 0, g], (((1,), (1,)), ((), ())), preferred_element_type=F32)
        s = jnp.where(cmp_ok, s, MASKED)
        m = jnp.max(s, axis=-1, keepdims=True)
        p = jnp.where(cmp_ok, jnp.exp(s - m), 0.0)
        den = jnp.sum(p, axis=-1, keepdims=True)
        p = p / jnp.where(den > 0, den, 1.0)
        o = jnp.dot(p.astype(BF16), vc_ref[0, 0, g], preferred_element_type=F32)
        for i in range(r):
            oc_ref[:, (g * r + i) * HEAD_DIM:(g * r + i + 1) * HEAD_DIM] = o[i * tq:(i + 1) * tq].astype(oc_ref.dtype)
        psum = p[0:tq]
        for i in range(1, r):
            psum = psum + p[i * tq:(i + 1) * tq]
        imp = sum(jnp.dot(piece, map_ref[...], preferred_element_type=F32) for piece in _split3(psum))
        score = jnp.where(valid, imp + jnp.where(forced, FORCE_BONUS, 0.0), NEG_SCORE)
        st = score.T[0:n_sel]
        rank = jnp.zeros((n_sel, tq), F32)
        for i in range(n_sel):
            ri = st[i:i + 1]
            beats = jnp.where(ri > st, 1.0, jnp.where((ri == st) & (jsub > i), 1.0, 0.0))
            rank = rank + beats
        sel_rows.append(jnp.where(rank < top_n, 1.0, 0.0))
    pad = LANES - NSA_KV_HEADS * n_sel
    if pad:
        sel_rows.append(jnp.zeros((pad, tq), F32))
    sel_ref[...] = jnp.concatenate(sel_rows, axis=0).T.astype(sel_ref.dtype)


def _sel_map(ncmp_pad, n_sel):
    cst = np.arange(ncmp_pad)[:, None] * CMP_STRIDE
    sst = np.arange(LANES)[None, :] * SEL_BLOCK
    ov = np.clip(np.minimum(cst + CMP_BLOCK, sst + SEL_BLOCK) - np.maximum(cst, sst), 0, None) / CMP_BLOCK
    ov[:, n_sel:] = 0.0
    ov[ncmp_pad - 1:, :] = 0.0
    return jnp.asarray(ov, BF16)


def _cmp_attn(qn, kvc, bsz, t):
    n, qw = qn.shape
    ncmp = kvc.shape[3]
    n_sel = t // SEL_BLOCK
    top_n = min(SEL_TOPN, n_sel)
    assert NSA_KV_HEADS * n_sel <= LANES and ncmp <= LANES
    tq = _pick(t, 256)
    nq = t // tq
    kv_spec = lambda kv: pl.BlockSpec((1, 1, NSA_KV_HEADS, ncmp, HEAD_DIM), lambda b, i: (kv, b, 0, 0, 0))
    return pl.pallas_call(
        functools.partial(_cmp_attn_kernel, n_sel=n_sel, top_n=top_n),
        grid=(bsz, nq),
        in_specs=[pl.BlockSpec((tq, qw), lambda b, i: (b * nq + i, 0)),
                  kv_spec(0), kv_spec(1),
                  pl.BlockSpec((ncmp, LANES), lambda b, i: (0, 0))],
        out_specs=[pl.BlockSpec((tq, qw), lambda b, i: (b * nq + i, 0)),
                   pl.BlockSpec((tq, LANES), lambda b, i: (b * nq + i, 0))],
        out_shape=[jax.ShapeDtypeStruct((n, qw), BF16), jax.ShapeDtypeStruct((n, LANES), BF16)],
        compiler_params=_params("parallel", "parallel"),
        name="cmp_attn",
    )(qn, kvc, kvc, _sel_map(ncmp, n_sel))


def _masked_attention(q, k, v, bias, tq):
    s = lax.dot_general(q, k, (((1,), (1,)), ((), ())), preferred_element_type=F32)
    probs, dens = [], []
    for h in range(q.shape[0] // tq):
        sh = s[h * tq:(h + 1) * tq] + bias
        p = jnp.exp(sh - jnp.max(sh, axis=-1, keepdims=True))
        dens.append(jnp.sum(p, axis=-1, keepdims=True))
        probs.append(p.astype(BF16))
    o = jnp.dot(jnp.concatenate(probs, axis=0), v, preferred_element_type=F32)
    return o / jnp.concatenate(dens, axis=0)


def _sw_attn_kernel(q_ref, ks_ref, vs_ref, kw_ref, vw_ref, sel_ref, oc_ref, gate_ref, o_ref, *, n_sel):
    tq = q_ref.shape[0]
    r = NSA_GROUP
    g = pl.program_id(1)
    i = pl.program_id(2)
    nq = ks_ref.shape[0] // tq
    win_tiles = -(-(WINDOW - 1) // tq)

    def variant(ii):
        q = jnp.concatenate([q_ref[:, h * HEAD_DIM:(h + 1) * HEAD_DIM] for h in range(r)], axis=0)
        tpos = ii * tq + lax.broadcasted_iota(jnp.int32, (tq, 1), 0)
        hi = (ii + 1) * tq
        kpos = lax.broadcasted_iota(jnp.int32, (1, hi), 1)
        crow = lax.broadcasted_iota(jnp.int32, (LANES, 1), 0)
        expand = jnp.where(crow == g * n_sel + kpos // SEL_BLOCK, 1.0, 0.0).astype(BF16)
        chosen = jnp.dot(sel_ref[...], expand, preferred_element_type=F32)
        bias_s = jnp.where(kpos <= tpos, jnp.where(chosen > 0.5, 0.0, MASKED), MASKED)
        o_s = _masked_attention(q, ks_ref[0:hi, :], vs_ref[0:hi, :], bias_s, tq)
        lo = max(ii - win_tiles, 0) * tq
        dist = tpos - (lo + lax.broadcasted_iota(jnp.int32, (1, hi - lo), 1))
        bias_w = jnp.where(dist >= 0, jnp.where(dist < WINDOW, 0.0, MASKED), MASKED)
        o_w = _masked_attention(q, kw_ref[lo:hi, :], vw_ref[lo:hi, :], bias_w, tq)
        gate = jax.nn.sigmoid(gate_ref[...])
        for h in range(r):
            rows = slice(h * tq, (h + 1) * tq)
            cols = slice(h * HEAD_DIM, (h + 1) * HEAD_DIM)
            o = (gate[:, 3 * h:3 * h + 1] * oc_ref[:, cols].astype(F32)
                 + gate[:, 3 * h + 1:3 * h + 2] * o_s[rows]
                 + gate[:, 3 * h + 2:3 * h + 3] * o_w[rows])
            o_ref[:, cols] = o.astype(o_ref.dtype)

    for ii in range(nq):
        pl.when(i == ii)(functools.partial(variant, ii))


def _sw_attn(qr, kr, proj, sel, o_c, gates, bsz, t, vsel_col, vwin_col):
    n, qw = qr.shape
    g = NSA_KV_HEADS
    gw = NSA_GROUP * HEAD_DIM
    tq = _pick(t, 256)
    nq = t // tq
    n_sel = t // SEL_BLOCK
    vs0, vw0 = vsel_col // HEAD_DIM, vwin_col // HEAD_DIM
    qspec = pl.BlockSpec((tq, gw), lambda b, gi, i: (b * nq + i, gi))
    return pl.pallas_call(
        functools.partial(_sw_attn_kernel, n_sel=n_sel),
        grid=(bsz, g, nq),
        in_specs=[qspec,
                  pl.BlockSpec((t, HEAD_DIM), lambda b, gi, i: (b, gi)),
                  pl.BlockSpec((t, HEAD_DIM), lambda b, gi, i: (b, vs0 + gi)),
                  pl.BlockSpec((t, HEAD_DIM), lambda b, gi, i: (b, g + gi)),
                  pl.BlockSpec((t, HEAD_DIM), lambda b, gi, i: (b, vw0 + gi)),
                  pl.BlockSpec((tq, LANES), lambda b, gi, i: (b * nq + i, 0)),
                  qspec,
                  pl.BlockSpec((tq, LANES), lambda b, gi, i: (b * nq + i, gi))],
        out_specs=qspec,
        out_shape=jax.ShapeDtypeStruct((n, qw), BF16),
        compiler_params=_params("parallel", "parallel", "parallel"),
        name="sw_attn",
    )(qr, kr, proj, kr, proj, sel, o_c, gates)


def _outproj_kernel(a1_ref, a2_ref, w1_ref, w2_ref, x_ref, g_ref, o_ref):
    y = jnp.dot(a1_ref[...], w1_ref[...], preferred_element_type=F32)
    y = y + jnp.dot(a2_ref[...], w2_ref[...], preferred_element_type=F32)
    o_ref[...] = x_ref[...] + g_ref[0] * y


def _outproj(y_pool, y_nsa, w_out, x2d, gate, bsz, t):
    n, d = x2d.shape
    k1, k2 = y_pool.shape[1], y_nsa.shape[1]
    tm = _pick(t, 1024)
    tn = _pick(d, 1024)
    nt = t // tm
    w = w_out.astype(BF16)
    return pl.pallas_call(
        _outproj_kernel,
        grid=(n // tm, d // tn),
        in_specs=[pl.BlockSpec((tm, k1), lambda i, j: (i, 0)),
                  pl.BlockSpec((tm, k2), lambda i, j: (i, 0)),
                  pl.BlockSpec((k1, tn), lambda i, j: (0, j)),
                  pl.BlockSpec((k2, tn), lambda i, j: (0, j)),
                  pl.BlockSpec((tm, tn), lambda i, j: (i, j)),
                  pl.BlockSpec((1, 1, tn), lambda i, j: (i // nt, 0, j))],
        out_specs=pl.BlockSpec((tm, tn), lambda i, j: (i, j)),
        out_shape=jax.ShapeDtypeStruct((n, d), F32),
        compiler_params=_params("parallel", "parallel"),
        name="outproj",
    )(y_pool, y_nsa, w[:k1], w[k1:], x2d, gate.reshape(bsz, 1, d))


def _cmpx(v, i, j):
    hi = jnp.maximum(v[i], v[j])
    lo = jnp.minimum(v[i], v[j])
    v[i], v[j] = hi, lo


def _bitonic_merge_desc(v):
    n = len(v)
    d = n // 2
    while d >= 1:
        for i in range(n):
            if (i // d) % 2 == 0:
                _cmpx(v, i, i + d)
        d //= 2


def _sort_desc(v):
    n = len(v)
    size = 2
    while size <= n:
        for base in range(0, n, size):
            blk = v[base:base + size]
            blk[size // 2:] = blk[size // 2:][::-1]
            _bitonic_merge_desc(blk)
            v[base:base + size] = blk
        size *= 2


def _merge_top(a, b):
    n = len(a)
    v = [jnp.maximum(a[i], b[n - 1 - i]) for i in range(n)]
    _bitonic_merge_desc(v)
    return v


def _top_sublanes(s, k):
    nk = s.shape[0]
    groups = nk // 8
    assert groups == k
    v = [s[8 * i:8 * i + 8] for i in range(groups)]
    _sort_desc(v)
    shift = 4
    while shift >= 1:
        v = _merge_top(v, [pltpu.roll(x, shift, axis=0) for x in v])
        shift //= 2
    return v


def _route_kernel(qt_ref, keys_ref, st_ref, stat_ref):
    k = PEER_TOPK
    nh = keys_ref.shape[0]
    kd = keys_ref.shape[3]
    neg = jnp.full((8, qt_ref.shape[1]), -jnp.inf, F32)
    thr_rows, m0_rows, m1_rows, iz_rows = [], [], [], []
    for h in range(nh):
        tops = []
        for c in range(2):
            row0 = (h * 2 + c) * kd
            qa, qb, qc = _split3(qt_ref[row0:row0 + kd, :])
            ka, kb, _ = _split3(keys_ref[h, c])
            s = (jnp.dot(ka, qa, preferred_element_type=F32) + jnp.dot(ka, qb, preferred_element_type=F32)
                 + jnp.dot(kb, qa, preferred_element_type=F32) + jnp.dot(ka, qc, preferred_element_type=F32)
                 + jnp.dot(kb, qb, preferred_element_type=F32))
            st_ref[(h * 2 + c) * PEER_NKEYS:(h * 2 + c + 1) * PEER_NKEYS, :] = s
            tops.append(_top_sublanes(s, k))
        v0, v1 = tops
        cands = [v0[a] + v1[b] for a in range(k) for b in range(k) if (a + 1) * (b + 1) <= k]
        lists = []
        for base in range(0, len(cands), k):
            blk = cands[base:base + k]
            blk = blk + [neg] * (k - len(blk))
            _sort_desc(blk)
            lists.append(blk)
        best = lists[0]
        for other in lists[1:]:
            best = _merge_top(best, other)
        z = sum(jnp.exp(b - best[0]) for b in best)
        thr_rows.append(best[k - 1][0:1])
        m0_rows.append(v0[0][0:1])
        m1_rows.append(v1[0][0:1])
        iz_rows.append((1.0 / z)[0:1])
    stat_ref[...] = jnp.concatenate(thr_rows + m0_rows + m1_rows + iz_rows, axis=0)


def _route(qt, peer_keys):
    rows, n = qt.shape
    nh, _, nk, kd = peer_keys.shape
    assert nk == PEER_NKEYS and nk == 8 * PEER_TOPK
    tr = _pick(n, 512)
    return pl.pallas_call(
        _route_kernel,
        grid=(n // tr,),
        in_specs=[pl.BlockSpec((rows, tr), lambda i: (0, i)),
                  pl.BlockSpec((nh, 2, nk, kd), lambda i: (0, 0, 0, 0))],
        out_specs=[pl.BlockSpec((nh * 2 * nk, tr), lambda i: (0, i)),
                   pl.BlockSpec((4 * nh, tr), lambda i: (0, i))],
        out_shape=[jax.ShapeDtypeStruct((nh * 2 * nk, n), F32), jax.ShapeDtypeStruct((4 * nh, n), F32)],
        compiler_params=_params("parallel"),
        name="route",
    )(qt, peer_keys)


def _peer_kernel(ht_ref, u_ref, v_ref, st_ref, stat_ref, o_ref, e0_ref, e1_ref, wa_ref, new_ref, pre_ref,
                 pre_new_ref, rows_ref, *, n_tiles, n_pairs):
    s = pl.program_id(0)
    te = u_ref.shape[0]
    nk = PEER_NKEYS
    nh = e0_ref.shape[0]
    per_step = te // nk
    pair1 = jnp.clip(s - 1, 0, n_pairs - 1)

    @pl.when(s == 0)
    def _():
        o_ref[...] = jnp.zeros_like(o_ref)
        wa_ref[...] = jnp.zeros_like(wa_ref)
        pre_ref[...] = jnp.zeros_like(pre_ref)

    @pl.when((s == 0) | ((s >= 1) & ((s - 1) % n_tiles == 0)))
    def _():
        for h in range(nh):
            m0 = stat_ref[nh + h:nh + h + 1, :]
            m1 = stat_ref[2 * nh + h:2 * nh + h + 1, :]
            iz = stat_ref[3 * nh + h:3 * nh + h + 1, :]
            e0_ref[h] = jnp.exp(st_ref[(2 * h) * nk:(2 * h + 1) * nk, :] - m0)
            e1_ref[h] = jnp.exp(st_ref[(2 * h + 1) * nk:(2 * h + 2) * nk, :] - m1) * iz

    @pl.when((s >= 2) & ((s - 2) % n_tiles == 0))
    def _():
        o_ref[...] = jnp.zeros_like(o_ref)

    tm = ht_ref.shape[1]
    d = v_ref.shape[1]
    tile = pair1 % n_tiles
    n_split = 2
    th = tm // n_split
    dq = d // 4

    def first_matmul(half, row0):
        cols = slice(half * th, (half + 1) * th)
        pre_new_ref[:, cols] = jnp.dot(u_ref[pl.ds(row0, te), :], ht_ref[:, cols], preferred_element_type=F32)

    def second_matmul(q, row0):
        cols = slice(q * dq, (q + 1) * dq)
        o_ref[:, cols] += jnp.dot(wa_ref[pl.ds(row0, tm), :], v_ref[:, cols], preferred_element_type=F32)

    for a in range(per_step):
        i0 = tile * per_step + a
        for h in range(nh):
            k = (a * nh + h) * 2
            rows_ref[k:k + 1, :] = st_ref[pl.ds((2 * h) * nk + i0, 1), :]
            rows_ref[k + 1:k + 2, :] = e0_ref[h, pl.ds(i0, 1), :]

    def gate_piece(a, half):
        cols = slice(half * th, (half + 1) * th)
        w = None
        for h in range(nh):
            k = (a * nh + h) * 2
            s0 = rows_ref[k:k + 1, cols]
            e0 = rows_ref[k + 1:k + 2, cols]
            ssum = s0 + st_ref[(2 * h + 1) * nk:(2 * h + 2) * nk, cols]
            wh = jnp.where(ssum >= stat_ref[h:h + 1, cols], e0 * e1_ref[h, :, cols], 0.0)
            w = wh if w is None else w + wh
        prod = w * _gelu(pre_ref[a * nk:(a + 1) * nk, cols])
        new_ref[cols, a * nk:(a + 1) * nk] = prod.T.astype(BF16)
        folded = jnp.max(prod.reshape(nk // 8, 8, th), axis=0)
        return functools.reduce(jnp.maximum, [folded[:, c * LANES:(c + 1) * LANES] for c in range(th // LANES)])

    def zero_after(vregs):
        bits = pltpu.bitcast(functools.reduce(jnp.maximum, vregs), jnp.uint32)
        zero = lax.shift_right_logical(lax.shift_right_logical(bits, jnp.uint32(16)), jnp.uint32(16))
        return pl.multiple_of(zero[0, 0].astype(jnp.int32), 16)

    pieces = [(a, half) for a in range(per_step) for half in range(n_split)]
    per_chunk = -(-len(pieces) // 4)
    chunks = [functools.partial(first_matmul, 1)] + [functools.partial(second_matmul, q) for q in range(3)]
    first_matmul(0, 0)
    for ci, chunk in enumerate(chunks):
        chunk(zero_after([gate_piece(a, half) for a, half in pieces[ci * per_chunk:(ci + 1) * per_chunk]]))
    second_matmul(3, 0)
    wa_ref[...] = new_ref[...]
    pre_ref[...] = pre_new_ref[...]


def _peer(ht, u, v, st, stats):
    d, n = ht.shape
    e = u.shape[0]
    nh = stats.shape[0] // 4
    tm = _pick(n, 512)
    te = _pick(e, 512)
    n_tiles = e // te
    n_pairs = (n // tm) * n_tiles
    pair = lambda s, lag: jnp.clip(s - lag, 0, n_pairs - 1)
    return pl.pallas_call(
        functools.partial(_peer_kernel, n_tiles=n_tiles, n_pairs=n_pairs),
        grid=(n_pairs + 2,),
        in_specs=[pl.BlockSpec((d, tm), lambda s: (0, pair(s, 0) // n_tiles)),
                  pl.BlockSpec((te, d), lambda s: (pair(s, 0) % n_tiles, 0)),
                  pl.BlockSpec((te, d), lambda s: (pair(s, 2) % n_tiles, 0)),
                  pl.BlockSpec((st.shape[0], tm), lambda s: (0, pair(s, 1) // n_tiles)),
                  pl.BlockSpec((stats.shape[0], tm), lambda s: (0, pair(s, 1) // n_tiles))],
        out_specs=pl.BlockSpec((tm, d), lambda s: (pair(s, 2) // n_tiles, 0)),
        out_shape=jax.ShapeDtypeStruct((n, d), F32),
        scratch_shapes=[pltpu.VMEM((nh, PEER_NKEYS, tm), F32), pltpu.VMEM((nh, PEER_NKEYS, tm), F32),
                        pltpu.VMEM((tm, te), BF16), pltpu.VMEM((tm, te), BF16),
                        pltpu.VMEM((te, tm), F32), pltpu.VMEM((te, tm), F32),
                        pltpu.VMEM((2 * nh * (te // PEER_NKEYS), tm), F32)],
        compiler_params=_params("arbitrary"),
        name="peer",
    )(ht, u, v, st, stats)


def _combine_kernel(x_ref, y_ref, g_ref, o_ref):
    o_ref[0] = x_ref[0] + g_ref[0] * y_ref[0]


def _combine(x, y, gate):
    bsz, t, d = x.shape
    tt = _pick(t, 256)
    spec = pl.BlockSpec((1, tt, d), lambda b, i: (b, i, 0))
    return pl.pallas_call(
        _combine_kernel,
        grid=(bsz, t // tt),
        in_specs=[spec, spec, pl.BlockSpec((1, 1, d), lambda b, i: (b, 0, 0))],
        out_specs=spec,
        out_shape=jax.ShapeDtypeStruct((bsz, t, d), F32),
        compiler_params=_params("parallel", "parallel"),
        name="combine",
    )(x, y, gate.reshape(bsz, 1, d))


def _layer(x, c, w_ada, b_ada, norm1_g, norm2_g, w_in, w_out, w_pool, pool_scale, q_norm_g, k_norm_g,
           cmp_pe, cmp_w1, cmp_w2, w_pq, peer_keys, peer_u, peer_v):
    bsz, t, d = x.shape
    n = bsz * t
    g, r = NSA_KV_HEADS, NSA_GROUP
    pool_w = w_pool.shape[0] * w_pool.shape[1]
    nsa_w = g * r * HEAD_DIM
    kv_w = N_BRANCH * 2 * g * HEAD_DIM
    gate_w = g * r * N_BRANCH
    assert w_in.shape[1] == pool_w + nsa_w + kv_w + gate_w
    q_col, kv_col = pool_w, pool_w + nsa_w
    branch_w = g * HEAD_DIM
    assert pool_w % nsa_w in (0, pool_w) and q_col % 1024 == 0

    mod = _ada(c, w_ada, b_ada)
    shift1, scale1, gate1, shift2, scale2, gate2 = [mod[:, i * d:(i + 1) * d] for i in range(N_MOD)]

    h = _modnorm(x, norm1_g, scale1, shift1, transpose=False).reshape(n, d)
    main_w = pool_w + nsa_w + kv_w
    w_main = w_in[:, :main_w].astype(BF16)
    w_gate = w_in[:, main_w:].reshape(d, g, r * N_BRANCH)
    w_gate = jnp.pad(w_gate, ((0, 0), (0, 0), (0, LANES - r * N_BRANCH))).reshape(d, g * LANES).astype(BF16)
    proj = _matmul(h, w_main, BF16, "inproj")
    gates = _matmul(h, w_gate, F32, "gateproj")

    y_pool = _pool(proj, w_pool, pool_scale, bsz, t)

    scale = HEAD_DIM ** -0.5
    qblk = 1024
    qn, qr = _headnorm(proj, q_norm_g.reshape(1, 1, HEAD_DIM), t, qblk, lambda j: q_col // qblk + j, nsa_w // qblk,
                       lambda j: 0, scale, True, "qprep")
    ksel_blk = (kv_col + 2 * branch_w) // branch_w
    kr = _headnorm(proj, k_norm_g.reshape(N_BRANCH, 1, HEAD_DIM), t, branch_w, lambda j: ksel_blk + 2 * j, 2,
                   lambda j: 1 + j, 1.0, False, "kprep")
    kvc = _compress(proj, cmp_pe, cmp_w1, cmp_w2, k_norm_g[0], bsz, t, kv_col)
    o_c, sel = _cmp_attn(qn, kvc, bsz, t)
    y_nsa = _sw_attn(qr, kr, proj, sel, o_c, gates, bsz, t, kv_col + 3 * branch_w, kv_col + 5 * branch_w)

    x1 = _outproj(y_pool, y_nsa, w_out, x.reshape(n, d), gate1, bsz, t)

    ht = _modnorm(x1.reshape(bsz, t, d), norm2_g, scale2, shift2, transpose=True)
    qt = _matmul(w_pq.T.astype(BF16), ht, F32, "peer_q")
    st, stats = _route(qt, peer_keys)
    y = _peer(ht, peer_u.astype(BF16), peer_v.astype(BF16), st, stats)
    return _combine(x1.reshape(bsz, t, d), y.reshape(bsz, t, d), gate2)


def kernel(x, c, w_ada, b_ada, norm1_g, norm2_g, w_in, w_out, w_pool, pool_scale, q_norm_g, k_norm_g, cmp_pe,
           cmp_w1, cmp_w2, w_pq, peer_keys, peer_u, peer_v):
    for l in range(w_ada.shape[0]):
        x = _layer(x, c, w_ada[l], b_ada[l], norm1_g[l], norm2_g[l], w_in[l], w_out[l], w_pool[l], pool_scale[l],
                   q_norm_g[l], k_norm_g[l], cmp_pe[l], cmp_w1[l], cmp_w2[l], w_pq[l], peer_keys[l], peer_u[l],
                   peer_v[l])
    return x
```

```python
import functools
import math

import numpy as np
import jax
import jax.numpy as jnp
from jax import lax
from jax.experimental import pallas as pl
from jax.experimental.pallas import tpu as pltpu

F32 = jnp.float32
BF16 = jnp.bfloat16

HEAD_DIM = 128
POOL_WINDOWS = (2, 4, 8, 16)
NSA_GROUP = 6
NSA_KV_HEADS = 4
N_BRANCH = 3
CMP_BLOCK = 32
CMP_STRIDE = 16
SEL_BLOCK = 64
SEL_TOPN = 16
WINDOW = 512
FORCE_BONUS = 1000.0
NEG_SCORE = -1e9
ROPE_THETA = 10000.0
PEER_HEADS = 8
PEER_NKEYS = 128
PEER_TOPK = 16
NORM_EPS = 1e-6
N_MOD = 6

V7X_VMEM_BYTES = 64 * 1024 * 1024
VMEM_LIMIT = V7X_VMEM_BYTES - 6 * 1024 * 1024
LANES = 128
MASKED = -1e30


def _params(*sem):
    return pltpu.CompilerParams(dimension_semantics=sem, vmem_limit_bytes=VMEM_LIMIT)


def _pick(n, pref):
    t = min(n, pref)
    while n % t:
        t //= 2
    return t


def _gelu(x):
    return 0.5 * x * (1.0 + lax.erf(x * (1.0 / math.sqrt(2.0))))


def _ada_kernel(c_ref, w_ref, b_ref, o_ref):
    c = c_ref[...]
    a = (c * jax.nn.sigmoid(c)).astype(BF16)
    o_ref[...] = jnp.dot(a, w_ref[...].astype(BF16), preferred_element_type=F32) + b_ref[...]


def _ada(c, w_ada, b_ada):
    bsz, d = c.shape
    n = w_ada.shape[1]
    tn = _pick(n, 512)
    return pl.pallas_call(
        _ada_kernel,
        grid=(n // tn,),
        in_specs=[pl.BlockSpec((bsz, d), lambda j: (0, 0)),
                  pl.BlockSpec((d, tn), lambda j: (0, j)),
                  pl.BlockSpec((1, tn), lambda j: (0, j))],
        out_specs=pl.BlockSpec((bsz, tn), lambda j: (0, j)),
        out_shape=jax.ShapeDtypeStruct((bsz, n), F32),
        compiler_params=_params("parallel"),
        name="ada",
    )(c, w_ada, b_ada.reshape(1, n))


def _modnorm_kernel(x_ref, g_ref, sc_ref, sh_ref, o_ref, *, transpose):
    x = x_ref[0]
    ms = jnp.mean(x * x, axis=-1, keepdims=True)
    y = x * lax.rsqrt(ms + NORM_EPS) * g_ref[...]
    y = y * (1.0 + sc_ref[0]) + sh_ref[0]
    if transpose:
        o_ref[...] = y.T.astype(BF16)
    else:
        o_ref[0] = y.astype(BF16)


def _modnorm(x, g, scale, shift, transpose):
    bsz, t, d = x.shape
    tt = _pick(t, 256)
    nt = t // tt
    if transpose:
        out_spec = pl.BlockSpec((d, tt), lambda b, i: (0, b * nt + i))
        out_shape = jax.ShapeDtypeStruct((d, bsz * t), BF16)
    else:
        out_spec = pl.BlockSpec((1, tt, d), lambda b, i: (b, i, 0))
        out_shape = jax.ShapeDtypeStruct((bsz, t, d), BF16)
    return pl.pallas_call(
        functools.partial(_modnorm_kernel, transpose=transpose),
        grid=(bsz, nt),
        in_specs=[pl.BlockSpec((1, tt, d), lambda b, i: (b, i, 0)),
                  pl.BlockSpec((1, d), lambda b, i: (0, 0)),
                  pl.BlockSpec((1, 1, d), lambda b, i: (b, 0, 0)),
                  pl.BlockSpec((1, 1, d), lambda b, i: (b, 0, 0))],
        out_specs=out_spec,
        out_shape=out_shape,
        compiler_params=_params("parallel", "parallel"),
        name="modnorm_t" if transpose else "modnorm",
    )(x, g.reshape(1, d), scale.reshape(bsz, 1, d), shift.reshape(bsz, 1, d))


def _mm_kernel(a_ref, w_ref, o_ref):
    o_ref[...] = jnp.dot(a_ref[...], w_ref[...], preferred_element_type=F32).astype(o_ref.dtype)


def _matmul(a, w, out_dtype, name):
    m, k = a.shape
    n = w.shape[1]
    tm = _pick(m, 1024)
    tn = _pick(n, 1024)
    return pl.pallas_call(
        _mm_kernel,
        grid=(m // tm, n // tn),
        in_specs=[pl.BlockSpec((tm, k), lambda i, j: (i, 0)),
                  pl.BlockSpec((k, tn), lambda i, j: (0, j))],
        out_specs=pl.BlockSpec((tm, tn), lambda i, j: (i, j)),
        out_shape=jax.ShapeDtypeStruct((m, n), out_dtype),
        compiler_params=_params("parallel", "parallel"),
        name=name,
    )(a, w)


def _pool_kernel(p_ref, w_ref, s_ref, o_ref):
    t = p_ref.shape[0]
    gw = w_ref.shape[1]
    row = lax.broadcasted_iota(jnp.int32, (t, 1), 0)

    def shifted(v, k):
        return jnp.where(row >= k, pltpu.roll(v, k, axis=0), 0.0)

    for gi, win in enumerate(POOL_WINDOWS):
        p = p_ref[:, gi * gw:(gi + 1) * gw].astype(F32)
        acc = p
        span = 1
        while span < win:
            acc = acc + shifted(acc, span)
            span *= 2
        count = jnp.minimum(row + 1, win).astype(F32)
        dlt = acc / count - p
        y = jnp.dot(dlt.astype(BF16), w_ref[gi], preferred_element_type=F32)
        o_ref[:, gi * gw:(gi + 1) * gw] = (y * s_ref[:, gi * gw:(gi + 1) * gw]).astype(o_ref.dtype)


def _pool(proj, w_pool, pool_scale, bsz, t):
    ng, gw, _ = w_pool.shape
    pw = ng * gw
    return pl.pallas_call(
        _pool_kernel,
        grid=(bsz,),
        in_specs=[pl.BlockSpec((t, pw), lambda b: (b, 0)),
                  pl.BlockSpec((ng, gw, gw), lambda b: (0, 0, 0)),
                  pl.BlockSpec((1, pw), lambda b: (0, 0))],
        out_specs=pl.BlockSpec((t, pw), lambda b: (b, 0)),
        out_shape=jax.ShapeDtypeStruct((bsz * t, pw), BF16),
        compiler_params=_params("parallel"),
        name="pool",
    )(proj, w_pool.astype(BF16), pool_scale.reshape(1, pw))


def _headnorm_kernel(x_ref, g_ref, cos_ref, sin_ref, *o_refs, n_heads, out_scale, with_plain):
    cosf = cos_ref[...]
    sinf = sin_ref[...]
    g = g_ref[0]
    for h in range(n_heads):
        sl = slice(h * HEAD_DIM, (h + 1) * HEAD_DIM)
        x = x_ref[:, sl].astype(F32)
        ms = jnp.mean(x * x, axis=-1, keepdims=True)
        y = x * lax.rsqrt(ms + NORM_EPS) * g
        rot = y * cosf + pltpu.roll(y, HEAD_DIM // 2, axis=1) * sinf
        if with_plain:
            o_refs[0][:, sl] = (y * out_scale).astype(BF16)
            o_refs[1][:, sl] = (rot * out_scale).astype(BF16)
        else:
            o_refs[0][:, sl] = (rot * out_scale).astype(BF16)


def _rope_tables(t):
    half = HEAD_DIM // 2
    inv = ROPE_THETA ** (-np.arange(half, dtype=np.float64) / half)
    ang = np.arange(t, dtype=np.float64)[:, None] * inv[None, :]
    cosf = np.concatenate([np.cos(ang), np.cos(ang)], axis=-1)
    sinf = np.concatenate([-np.sin(ang), np.sin(ang)], axis=-1)
    return jnp.asarray(cosf, F32), jnp.asarray(sinf, F32)


def _headnorm(proj, gamma, t, width, col_of, n_col_blocks, gamma_of, out_scale, with_plain, name):
    n = proj.shape[0]
    tt = _pick(t, 512)
    nt = t // tt
    cosf, sinf = _rope_tables(t)
    n_out = 2 if with_plain else 1
    out_w = n_col_blocks * width
    out = pl.pallas_call(
        functools.partial(_headnorm_kernel, n_heads=width // HEAD_DIM, out_scale=out_scale, with_plain=with_plain),
        grid=(n // tt, n_col_blocks),
        in_specs=[pl.BlockSpec((tt, width), lambda i, j: (i, col_of(j))),
                  pl.BlockSpec((1, 1, HEAD_DIM), lambda i, j: (gamma_of(j), 0, 0)),
                  pl.BlockSpec((tt, HEAD_DIM), lambda i, j: (i % nt, 0)),
                  pl.BlockSpec((tt, HEAD_DIM), lambda i, j: (i % nt, 0))],
        out_specs=[pl.BlockSpec((tt, width), lambda i, j: (i, j))] * n_out,
        out_shape=[jax.ShapeDtypeStruct((n, out_w), BF16)] * n_out,
        compiler_params=_params("parallel", "parallel"),
        name=name,
    )(proj, gamma, cosf, sinf)
    return out if with_plain else out[0]


def _compress_kernel(x_ref, pe_ref, w1_ref, w2_ref, g_ref, o_ref, xs_ref):
    t = x_ref.shape[0]
    nblk = t // CMP_STRIDE
    kv = pl.program_id(1)
    xs_ref[...] = x_ref[...].astype(F32)
    cols = [xs_ref[pl.ds(l, nblk, stride=CMP_STRIDE), :] for l in range(CMP_STRIDE)]
    xc = jnp.concatenate(cols, axis=1)
    lo = jnp.dot((xc + pe_ref[0, 0]).astype(BF16), w1_ref[0, 0], preferred_element_type=F32)
    hi = jnp.dot((xc + pe_ref[0, 1]).astype(BF16), w1_ref[0, 1], preferred_element_type=F32)
    pre = lo + pltpu.roll(hi, nblk - 1, axis=0)
    hid = _gelu(pre)
    out = jnp.dot(hid.astype(BF16), w2_ref[0], preferred_element_type=F32)
    ms = jnp.mean(out * out, axis=-1, keepdims=True)
    normed = out * lax.rsqrt(ms + NORM_EPS) * g_ref[...]
    out = jnp.where(kv == 0, normed, out)
    row = lax.broadcasted_iota(jnp.int32, out.shape, 0)
    o_ref[0, 0, 0] = jnp.where(row < nblk - 1, out, 0.0).astype(BF16)


def _compress(proj, cmp_pe, cmp_w1, cmp_w2, k_gamma, bsz, t, col0):
    g = NSA_KV_HEADS
    nblk = t // CMP_STRIDE
    hidden = cmp_w1.shape[-1]
    halves = CMP_BLOCK // CMP_STRIDE
    w1 = cmp_w1.astype(BF16).reshape(2, halves, CMP_STRIDE * HEAD_DIM, hidden)
    pe = cmp_pe.reshape(2, halves, 1, CMP_STRIDE * HEAD_DIM)
    cb0 = col0 // HEAD_DIM
    return pl.pallas_call(
        _compress_kernel,
        grid=(bsz, 2, g),
        in_specs=[pl.BlockSpec((t, HEAD_DIM), lambda b, kv, gi: (b, cb0 + kv * g + gi)),
                  pl.BlockSpec((1, halves, 1, CMP_STRIDE * HEAD_DIM), lambda b, kv, gi: (kv, 0, 0, 0)),
                  pl.BlockSpec((1, halves, CMP_STRIDE * HEAD_DIM, hidden), lambda b, kv, gi: (kv, 0, 0, 0)),
                  pl.BlockSpec((1, hidden, HEAD_DIM), lambda b, kv, gi: (kv, 0, 0)),
                  pl.BlockSpec((1, HEAD_DIM), lambda b, kv, gi: (0, 0))],
        out_specs=pl.BlockSpec((1, 1, 1, nblk, HEAD_DIM), lambda b, kv, gi: (kv, b, gi, 0, 0)),
        out_shape=jax.ShapeDtypeStruct((2, bsz, g, nblk, HEAD_DIM), BF16),
        scratch_shapes=[pltpu.VMEM((t, HEAD_DIM), F32)],
        compiler_params=_params("parallel", "parallel", "parallel"),
        name="compress",
    )(proj, pe, w1, cmp_w2.astype(BF16), k_gamma.reshape(1, HEAD_DIM))


def _split3(x):
    a = x.astype(BF16)
    r = x - a.astype(F32)
    b = r.astype(BF16)
    c = (r - b.astype(F32)).astype(BF16)
    return a, b, c


def _cmp_attn_kernel(q_ref, kc_ref, vc_ref, map_ref, oc_ref, sel_ref, *, n_sel, top_n):
    tq = q_ref.shape[0]
    ncmp = kc_ref.shape[3]
    r = NSA_GROUP
    t0 = pl.program_id(1) * tq
    tcol = t0 + lax.broadcasted_iota(jnp.int32, (tq, 1), 0)
    tcol_r = jnp.concatenate([tcol] * r, axis=0)
    nrow = lax.broadcasted_iota(jnp.int32, (1, ncmp), 1)
    cmp_ok = (nrow * CMP_STRIDE + (CMP_BLOCK - 1) <= tcol_r) & (nrow < ncmp - 1)
    jrow = lax.broadcasted_iota(jnp.int32, (1, ncmp), 1)
    cur = tcol // SEL_BLOCK
    forced = (jrow == 0) | (jrow == cur) | (jrow == cur - 1)
    valid = jrow * SEL_BLOCK <= tcol
    jsub = lax.broadcasted_iota(jnp.int32, (n_sel, tq), 0)
    sel_rows = []
    for g in range(NSA_KV_HEADS):
        q = jnp.concatenate(
            [q_ref[:, (g * r + i) * HEAD_DIM:(g * r + i + 1) * HEAD_DIM] for i in range(r)], axis=0)
        s = lax.dot_general(q, kc_ref[0, 0, g], (((1,), (1,)), ((), ())), preferred_element_type=F32)
        s = jnp.where(cmp_ok, s, MASKED)
        m = jnp.max(s, axis=-1, keepdims=True)
        p = jnp.where(cmp_ok, jnp.exp(s - m), 0.0)
        den = jnp.sum(p, axis=-1, keepdims=True)
        p = p / jnp.where(den > 0, den, 1.0)
        o = jnp.dot(p.astype(BF16), vc_ref[0, 0, g], preferred_element_type=F32)
        for i in range(r):
            oc_ref[:, (g * r + i) * HEAD_DIM:(g * r + i + 1) * HEAD_DIM] = o[i * tq:(i + 1) * tq].astype(oc_ref.dtype)
        psum = p[0:tq]
        for i in range(1, r):
            psum = psum + p[i * tq:(i + 1) * tq]
        imp = sum(jnp.dot(piece, map_ref[...], preferred_element_type=F32) for piece in _split3(psum))
        score = jnp.where(valid, imp + jnp.where(forced, FORCE_BONUS, 0.0), NEG_SCORE)
        st = score.T[0:n_sel]
        rank = jnp.zeros((n_sel, tq), F32)
        for i in range(n_sel):
            ri = st[i:i + 1]
            beats = jnp.where(ri > st, 1.0, jnp.where((ri == st) & (jsub > i), 1.0, 0.0))
            rank = rank + beats
        sel_rows.append(jnp.where(rank < top_n, 1.0, 0.0))
    pad = LANES - NSA_KV_HEADS * n_sel
    if pad:
        sel_rows.append(jnp.zeros((pad, tq), F32))
    sel_ref[...] = jnp.concatenate(sel_rows, axis=0).T.astype(sel_ref.dtype)


def _sel_map(ncmp_pad, n_sel):
    cst = np.arange(ncmp_pad)[:, None] * CMP_STRIDE
    sst = np.arange(LANES)[None, :] * SEL_BLOCK
    ov = np.clip(np.minimum(cst + CMP_BLOCK, sst + SEL_BLOCK) - np.maximum(cst, sst), 0, None) / CMP_BLOCK
    ov[:, n_sel:] = 0.0
    ov[ncmp_pad - 1:, :] = 0.0
    return jnp.asarray(ov, BF16)


def _cmp_attn(qn, kvc, bsz, t):
    n, qw = qn.shape
    ncmp = kvc.shape[3]
    n_sel = t // SEL_BLOCK
    top_n = min(SEL_TOPN, n_sel)
    assert NSA_KV_HEADS * n_sel <= LANES and ncmp <= LANES
    tq = _pick(t, 256)
    nq = t // tq
    kv_spec = lambda kv: pl.BlockSpec((1, 1, NSA_KV_HEADS, ncmp, HEAD_DIM), lambda b, i: (kv, b, 0, 0, 0))
    return pl.pallas_call(
        functools.partial(_cmp_attn_kernel, n_sel=n_sel, top_n=top_n),
        grid=(bsz, nq),
        in_specs=[pl.BlockSpec((tq, qw), lambda b, i: (b * nq + i, 0)),
                  kv_spec(0), kv_spec(1),
                  pl.BlockSpec((ncmp, LANES), lambda b, i: (0, 0))],
        out_specs=[pl.BlockSpec((tq, qw), lambda b, i: (b * nq + i, 0)),
                   pl.BlockSpec((tq, LANES), lambda b, i: (b * nq + i, 0))],
        out_shape=[jax.ShapeDtypeStruct((n, qw), BF16), jax.ShapeDtypeStruct((n, LANES), BF16)],
        compiler_params=_params("parallel", "parallel"),
        name="cmp_attn",
    )(qn, kvc, kvc, _sel_map(ncmp, n_sel))


def _two_pass_attention(q_ref, k_ref, v_ref, lo, hi, bias_fn, s_ref, m_ref, l_ref, acc_ref, tq, tk):
    rows = q_ref.shape[0]
    heads = rows // tq
    half = tk // 2
    m_ref[...] = jnp.full(m_ref.shape, MASKED, F32)

    def scores(j, carry):
        k = k_ref[pl.ds(pl.multiple_of(j * tk, tk), tk), :]
        s = lax.dot_general(q_ref[...], k, (((1,), (1,)), ((), ())), preferred_element_type=F32)
        bias = bias_fn(j)
        for h in range(heads):
            sl = slice(h * tq, (h + 1) * tq)
            sh = s[sl] + bias
            s_ref[j, sl, :] = sh
            m_ref[sl, :] = jnp.maximum(m_ref[sl, :], jnp.maximum(sh[:, :half], sh[:, half:]))
        return carry

    lax.fori_loop(lo, hi, scores, 0)
    m_ref[...] = jnp.broadcast_to(jnp.max(m_ref[...], axis=-1, keepdims=True), m_ref.shape)
    l_ref[...] = jnp.zeros(l_ref.shape, F32)
    acc_ref[...] = jnp.zeros(acc_ref.shape, F32)

    def values(j, carry):
        m = m_ref[...]
        s = s_ref[j]
        pa = jnp.exp(s[:, :half] - m)
        pb = jnp.exp(s[:, half:] - m)
        l_ref[...] += pa + pb
        p = jnp.concatenate([pa.astype(BF16), pb.astype(BF16)], axis=1)
        v = v_ref[pl.ds(pl.multiple_of(j * tk, tk), tk), :]
        acc_ref[...] += jnp.dot(p, v, preferred_element_type=F32)
        return carry

    lax.fori_loop(lo, hi, values, 0)
    return acc_ref[...] / jnp.sum(l_ref[...], axis=-1, keepdims=True)


def _sw_attn_kernel(q_ref, ks_ref, vs_ref, kw_ref, vw_ref, sel_ref, oc_ref, gate_ref, o_ref,
                    q6_ref, s_ref, m_ref, l_ref, acc_ref, *, n_sel):
    tq = q_ref.shape[0]
    tk = s_ref.shape[2]
    r = NSA_GROUP
    g = pl.program_id(1)
    t0 = pl.program_id(2) * tq
    for h in range(r):
        q6_ref[h * tq:(h + 1) * tq, :] = q_ref[:, h * HEAD_DIM:(h + 1) * HEAD_DIM]
    tpos = t0 + lax.broadcasted_iota(jnp.int32, (tq, 1), 0)
    lane = lax.broadcasted_iota(jnp.int32, (1, tk), 1)
    crow = lax.broadcasted_iota(jnp.int32, (LANES, 1), 0)

    def sel_bias(j):
        kpos = j * tk + lane
        expand = jnp.where(crow == g * n_sel + kpos // SEL_BLOCK, 1.0, 0.0).astype(BF16)
        chosen = jnp.dot(sel_ref[...], expand, preferred_element_type=F32)
        return jnp.where(kpos <= tpos, jnp.where(chosen > 0.5, 0.0, MASKED), MASKED)

    def win_bias(j):
        dist = tpos - (j * tk + lane)
        return jnp.where(dist >= 0, jnp.where(dist < WINDOW, 0.0, MASKED), MASKED)

    hi = (t0 + tq + tk - 1) // tk
    scratch = (s_ref, m_ref, l_ref, acc_ref, tq, tk)
    o_s = _two_pass_attention(q6_ref, ks_ref, vs_ref, 0, hi, sel_bias, *scratch)
    lo_w = jnp.maximum(t0 - (WINDOW - 1), 0) // tk
    o_w = _two_pass_attention(q6_ref, kw_ref, vw_ref, lo_w, hi, win_bias, *scratch)
    gate = jax.nn.sigmoid(gate_ref[...])
    for h in range(r):
        rows = slice(h * tq, (h + 1) * tq)
        cols = slice(h * HEAD_DIM, (h + 1) * HEAD_DIM)
        o = (gate[:, 3 * h:3 * h + 1] * oc_ref[:, cols].astype(F32)
             + gate[:, 3 * h + 1:3 * h + 2] * o_s[rows]
             + gate[:, 3 * h + 2:3 * h + 3] * o_w[rows])
        o_ref[:, cols] = o.astype(o_ref.dtype)


def _sw_attn(qr, kr, proj, sel, o_c, gates, bsz, t, vsel_col, vwin_col):
    n, qw = qr.shape
    g = NSA_KV_HEADS
    gw = NSA_GROUP * HEAD_DIM
    tq = _pick(t, 256)
    tk = tq
    rows = NSA_GROUP * tq
    nq = t // tq
    n_sel = t // SEL_BLOCK
    vs0, vw0 = vsel_col // HEAD_DIM, vwin_col // HEAD_DIM
    qspec = pl.BlockSpec((tq, gw), lambda b, gi, i: (b * nq + i, gi))
    return pl.pallas_call(
        functools.partial(_sw_attn_kernel, n_sel=n_sel),
        grid=(bsz, g, nq),
        in_specs=[qspec,
                  pl.BlockSpec((t, HEAD_DIM), lambda b, gi, i: (b, gi)),
                  pl.BlockSpec((t, HEAD_DIM), lambda b, gi, i: (b, vs0 + gi)),
                  pl.BlockSpec((t, HEAD_DIM), lambda b, gi, i: (b, g + gi)),
                  pl.BlockSpec((t, HEAD_DIM), lambda b, gi, i: (b, vw0 + gi)),
                  pl.BlockSpec((tq, LANES), lambda b, gi, i: (b * nq + i, 0)),
                  qspec,
                  pl.BlockSpec((tq, LANES), lambda b, gi, i: (b * nq + i, gi))],
        out_specs=qspec,
        out_shape=jax.ShapeDtypeStruct((n, qw), BF16),
        scratch_shapes=[pltpu.VMEM((rows, HEAD_DIM), BF16), pltpu.VMEM((t // tk, rows, tk), F32),
                        pltpu.VMEM((rows, tk // 2), F32), pltpu.VMEM((rows, tk // 2), F32),
                        pltpu.VMEM((rows, HEAD_DIM), F32)],
        compiler_params=_params("parallel", "parallel", "parallel"),
        name="sw_attn",
    )(qr, kr, proj, kr, proj, sel, o_c, gates)


def _outproj_kernel(a1_ref, a2_ref, w1_ref, w2_ref, x_ref, g_ref, o_ref):
    y = jnp.dot(a1_ref[...], w1_ref[...], preferred_element_type=F32)
    y = y + jnp.dot(a2_ref[...], w2_ref[...], preferred_element_type=F32)
    o_ref[...] = x_ref[...] + g_ref[0] * y


def _outproj(y_pool, y_nsa, w_out, x2d, gate, bsz, t):
    n, d = x2d.shape
    k1, k2 = y_pool.shape[1], y_nsa.shape[1]
    tm = _pick(t, 1024)
    tn = _pick(d, 1024)
    nt = t // tm
    w = w_out.astype(BF16)
    return pl.pallas_call(
        _outproj_kernel,
        grid=(n // tm, d // tn),
        in_specs=[pl.BlockSpec((tm, k1), lambda i, j: (i, 0)),
                  pl.BlockSpec((tm, k2), lambda i, j: (i, 0)),
                  pl.BlockSpec((k1, tn), lambda i, j: (0, j)),
                  pl.BlockSpec((k2, tn), lambda i, j: (0, j)),
                  pl.BlockSpec((tm, tn), lambda i, j: (i, j)),
                  pl.BlockSpec((1, 1, tn), lambda i, j: (i // nt, 0, j))],
        out_specs=pl.BlockSpec((tm, tn), lambda i, j: (i, j)),
        out_shape=jax.ShapeDtypeStruct((n, d), F32),
        compiler_params=_params("parallel", "parallel"),
        name="outproj",
    )(y_pool, y_nsa, w[:k1], w[k1:], x2d, gate.reshape(bsz, 1, d))


def _cmpx(v, i, j):
    hi = jnp.maximum(v[i], v[j])
    lo = jnp.minimum(v[i], v[j])
    v[i], v[j] = hi, lo


def _bitonic_merge_desc(v):
    n = len(v)
    d = n // 2
    while d >= 1:
        for i in range(n):
            if (i // d) % 2 == 0:
                _cmpx(v, i, i + d)
        d //= 2


def _sort_desc(v):
    n = len(v)
    size = 2
    while size <= n:
        for base in range(0, n, size):
            blk = v[base:base + size]
            blk[size // 2:] = blk[size // 2:][::-1]
            _bitonic_merge_desc(blk)
            v[base:base + size] = blk
        size *= 2


def _merge_top(a, b):
    n = len(a)
    v = [jnp.maximum(a[i], b[n - 1 - i]) for i in range(n)]
    _bitonic_merge_desc(v)
    return v


def _top_sublanes(s, k):
    nk = s.shape[0]
    groups = nk // 8
    assert groups == k
    v = [s[8 * i:8 * i + 8] for i in range(groups)]
    _sort_desc(v)
    shift = 4
    while shift >= 1:
        v = _merge_top(v, [pltpu.roll(x, shift, axis=0) for x in v])
        shift //= 2
    return v


def _route_kernel(qt_ref, keys_ref, st_ref, stat_ref):
    k = PEER_TOPK
    nh = keys_ref.shape[0]
    kd = keys_ref.shape[3]
    neg = jnp.full((8, qt_ref.shape[1]), -jnp.inf, F32)
    thr_rows, m0_rows, m1_rows, iz_rows = [], [], [], []
    for h in range(nh):
        tops = []
        for c in range(2):
            row0 = (h * 2 + c) * kd
            qa, qb, qc = _split3(qt_ref[row0:row0 + kd, :])
            ka, kb, _ = _split3(keys_ref[h, c])
            s = (jnp.dot(ka, qa, preferred_element_type=F32) + jnp.dot(ka, qb, preferred_element_type=F32)
                 + jnp.dot(kb, qa, preferred_element_type=F32) + jnp.dot(ka, qc, preferred_element_type=F32)
                 + jnp.dot(kb, qb, preferred_element_type=F32))
            st_ref[(h * 2 + c) * PEER_NKEYS:(h * 2 + c + 1) * PEER_NKEYS, :] = s
            tops.append(_top_sublanes(s, k))
        v0, v1 = tops
        cands = [v0[a] + v1[b] for a in range(k) for b in range(k) if (a + 1) * (b + 1) <= k]
        lists = []
        for base in range(0, len(cands), k):
            blk = cands[base:base + k]
            blk = blk + [neg] * (k - len(blk))
            _sort_desc(blk)
            lists.append(blk)
        best = lists[0]
        for other in lists[1:]:
            best = _merge_top(best, other)
        z = sum(jnp.exp(b - best[0]) for b in best)
        thr_rows.append(best[k - 1][0:1])
        m0_rows.append(v0[0][0:1])
        m1_rows.append(v1[0][0:1])
        iz_rows.append((1.0 / z)[0:1])
    stat_ref[...] = jnp.concatenate(thr_rows + m0_rows + m1_rows + iz_rows, axis=0)


def _route(qt, peer_keys):
    rows, n = qt.shape
    nh, _, nk, kd = peer_keys.shape
    assert nk == PEER_NKEYS and nk == 8 * PEER_TOPK
    tr = _pick(n, 512)
    return pl.pallas_call(
        _route_kernel,
        grid=(n // tr,),
        in_specs=[pl.BlockSpec((rows, tr), lambda i: (0, i)),
                  pl.BlockSpec((nh, 2, nk, kd), lambda i: (0, 0, 0, 0))],
        out_specs=[pl.BlockSpec((nh * 2 * nk, tr), lambda i: (0, i)),
                   pl.BlockSpec((4 * nh, tr), lambda i: (0, i))],
        out_shape=[jax.ShapeDtypeStruct((nh * 2 * nk, n), F32), jax.ShapeDtypeStruct((4 * nh, n), F32)],
        compiler_params=_params("parallel"),
        name="route",
    )(qt, peer_keys)


def _peer_kernel(ht_ref, u_ref, v_ref, st_ref, stat_ref, o_ref, e0_ref, e1_ref, wa_ref, new_ref, pre_ref,
                 pre_new_ref, rows_ref, *, n_tiles, n_pairs):
    s = pl.program_id(0)
    te = u_ref.shape[0]
    nk = PEER_NKEYS
    nh = e0_ref.shape[0]
    per_step = te // nk
    pair1 = jnp.clip(s - 1, 0, n_pairs - 1)

    @pl.when(s == 0)
    def _():
        o_ref[...] = jnp.zeros_like(o_ref)
        wa_ref[...] = jnp.zeros_like(wa_ref)
        pre_ref[...] = jnp.zeros_like(pre_ref)

    @pl.when((s == 0) | ((s >= 1) & ((s - 1) % n_tiles == 0)))
    def _():
        for h in range(nh):
            m0 = stat_ref[nh + h:nh + h + 1, :]
            m1 = stat_ref[2 * nh + h:2 * nh + h + 1, :]
            iz = stat_ref[3 * nh + h:3 * nh + h + 1, :]
            e0_ref[h] = jnp.exp(st_ref[(2 * h) * nk:(2 * h + 1) * nk, :] - m0)
            e1_ref[h] = jnp.exp(st_ref[(2 * h + 1) * nk:(2 * h + 2) * nk, :] - m1) * iz

    @pl.when((s >= 2) & ((s - 2) % n_tiles == 0))
    def _():
        o_ref[...] = jnp.zeros_like(o_ref)

    tm = ht_ref.shape[1]
    d = v_ref.shape[1]
    tile = pair1 % n_tiles
    n_split = 2
    th = tm // n_split
    dq = d // 4

    def first_matmul(half, row0):
        cols = slice(half * th, (half + 1) * th)
        pre_new_ref[:, cols] = jnp.dot(u_ref[pl.ds(row0, te), :], ht_ref[:, cols], preferred_element_type=F32)

    def second_matmul(q, row0):
        cols = slice(q * dq, (q + 1) * dq)
        o_ref[:, cols] += jnp.dot(wa_ref[pl.ds(row0, tm), :], v_ref[:, cols], preferred_element_type=F32)

    for a in range(per_step):
        i0 = tile * per_step + a
        for h in range(nh):
            k = (a * nh + h) * 2
            rows_ref[k:k + 1, :] = st_ref[pl.ds((2 * h) * nk + i0, 1), :]
            rows_ref[k + 1:k + 2, :] = e0_ref[h, pl.ds(i0, 1), :]

    def gate_piece(a, half):
        cols = slice(half * th, (half + 1) * th)
        w = None
        for h in range(nh):
            k = (a * nh + h) * 2
            s0 = rows_ref[k:k + 1, cols]
            e0 = rows_ref[k + 1:k + 2, cols]
            ssum = s0 + st_ref[(2 * h + 1) * nk:(2 * h + 2) * nk, cols]
            wh = jnp.where(ssum >= stat_ref[h:h + 1, cols], e0 * e1_ref[h, :, cols], 0.0)
            w = wh if w is None else w + wh
        prod = w * _gelu(pre_ref[a * nk:(a + 1) * nk, cols])
        new_ref[cols, a * nk:(a + 1) * nk] = prod.T.astype(BF16)
        folded = jnp.max(prod.reshape(nk // 8, 8, th), axis=0)
        return functools.reduce(jnp.maximum, [folded[:, c * LANES:(c + 1) * LANES] for c in range(th // LANES)])

    def zero_after(vregs):
        bits = pltpu.bitcast(functools.reduce(jnp.maximum, vregs), jnp.uint32)
        zero = lax.shift_right_logical(lax.shift_right_logical(bits, jnp.uint32(16)), jnp.uint32(16))
        return pl.multiple_of(zero[0, 0].astype(jnp.int32), 16)

    pieces = [(a, half) for a in range(per_step) for half in range(n_split)]
    per_chunk = -(-len(pieces) // 4)
    chunks = [functools.partial(first_matmul, 1)] + [functools.partial(second_matmul, q) for q in range(3)]
    first_matmul(0, 0)
    for ci, chunk in enumerate(chunks):
        chunk(zero_after([gate_piece(a, half) for a, half in pieces[ci * per_chunk:(ci + 1) * per_chunk]]))
    second_matmul(3, 0)
    wa_ref[...] = new_ref[...]
    pre_ref[...] = pre_new_ref[...]


def _peer(ht, u, v, st, stats):
    d, n = ht.shape
    e = u.shape[0]
    nh = stats.shape[0] // 4
    tm = _pick(n, 512)
    te = _pick(e, 512)
    n_tiles = e // te
    n_pairs = (n // tm) * n_tiles
    pair = lambda s, lag: jnp.clip(s - lag, 0, n_pairs - 1)
    return pl.pallas_call(
        functools.partial(_peer_kernel, n_tiles=n_tiles, n_pairs=n_pairs),
        grid=(n_pairs + 2,),
        in_specs=[pl.BlockSpec((d, tm), lambda s: (0, pair(s, 0) // n_tiles)),
                  pl.BlockSpec((te, d), lambda s: (pair(s, 0) % n_tiles, 0)),
                  pl.BlockSpec((te, d), lambda s: (pair(s, 2) % n_tiles, 0)),
                  pl.BlockSpec((st.shape[0], tm), lambda s: (0, pair(s, 1) // n_tiles)),
                  pl.BlockSpec((stats.shape[0], tm), lambda s: (0, pair(s, 1) // n_tiles))],
        out_specs=pl.BlockSpec((tm, d), lambda s: (pair(s, 2) // n_tiles, 0)),
        out_shape=jax.ShapeDtypeStruct((n, d), F32),
        scratch_shapes=[pltpu.VMEM((nh, PEER_NKEYS, tm), F32), pltpu.VMEM((nh, PEER_NKEYS, tm), F32),
                        pltpu.VMEM((tm, te), BF16), pltpu.VMEM((tm, te), BF16),
                        pltpu.VMEM((te, tm), F32), pltpu.VMEM((te, tm), F32),
                        pltpu.VMEM((2 * nh * (te // PEER_NKEYS), tm), F32)],
        compiler_params=_params("arbitrary"),
        name="peer",
    )(ht, u, v, st, stats)


def _combine_kernel(x_ref, y_ref, g_ref, o_ref):
    o_ref[0] = x_ref[0] + g_ref[0] * y_ref[0]


def _combine(x, y, gate):
    bsz, t, d = x.shape
    tt = _pick(t, 256)
    spec = pl.BlockSpec((1, tt, d), lambda b, i: (b, i, 0))
    return pl.pallas_call(
        _combine_kernel,
        grid=(bsz, t // tt),
        in_specs=[spec, spec, pl.BlockSpec((1, 1, d), lambda b, i: (b, 0, 0))],
        out_specs=spec,
        out_shape=jax.ShapeDtypeStruct((bsz, t, d), F32),
        compiler_params=_params("parallel", "parallel"),
        name="combine",
    )(x, y, gate.reshape(bsz, 1, d))


def _layer(x, c, w_ada, b_ada, norm1_g, norm2_g, w_in, w_out, w_pool, pool_scale, q_norm_g, k_norm_g,
           cmp_pe, cmp_w1, cmp_w2, w_pq, peer_keys, peer_u, peer_v):
    bsz, t, d = x.shape
    n = bsz * t
    g, r = NSA_KV_HEADS, NSA_GROUP
    pool_w = w_pool.shape[0] * w_pool.shape[1]
    nsa_w = g * r * HEAD_DIM
    kv_w = N_BRANCH * 2 * g * HEAD_DIM
    gate_w = g * r * N_BRANCH
    assert w_in.shape[1] == pool_w + nsa_w + kv_w + gate_w
    q_col, kv_col = pool_w, pool_w + nsa_w
    branch_w = g * HEAD_DIM
    assert pool_w % nsa_w in (0, pool_w) and q_col % 1024 == 0

    mod = _ada(c, w_ada, b_ada)
    shift1, scale1, gate1, shift2, scale2, gate2 = [mod[:, i * d:(i + 1) * d] for i in range(N_MOD)]

    h = _modnorm(x, norm1_g, scale1, shift1, transpose=False).reshape(n, d)
    main_w = pool_w + nsa_w + kv_w
    w_main = w_in[:, :main_w].astype(BF16)
    w_gate = w_in[:, main_w:].reshape(d, g, r * N_BRANCH)
    w_gate = jnp.pad(w_gate, ((0, 0), (0, 0), (0, LANES - r * N_BRANCH))).reshape(d, g * LANES).astype(BF16)
    proj = _matmul(h, w_main, BF16, "inproj")
    gates = _matmul(h, w_gate, F32, "gateproj")

    y_pool = _pool(proj, w_pool, pool_scale, bsz, t)

    scale = HEAD_DIM ** -0.5
    qblk = 1024
    qn, qr = _headnorm(proj, q_norm_g.reshape(1, 1, HEAD_DIM), t, qblk, lambda j: q_col // qblk + j, nsa_w // qblk,
                       lambda j: 0, scale, True, "qprep")
    ksel_blk = (kv_col + 2 * branch_w) // branch_w
    kr = _headnorm(proj, k_norm_g.reshape(N_BRANCH, 1, HEAD_DIM), t, branch_w, lambda j: ksel_blk + 2 * j, 2,
                   lambda j: 1 + j, 1.0, False, "kprep")
    kvc = _compress(proj, cmp_pe, cmp_w1, cmp_w2, k_norm_g[0], bsz, t, kv_col)
    o_c, sel = _cmp_attn(qn, kvc, bsz, t)
    y_nsa = _sw_attn(qr, kr, proj, sel, o_c, gates, bsz, t, kv_col + 3 * branch_w, kv_col + 5 * branch_w)

    x1 = _outproj(y_pool, y_nsa, w_out, x.reshape(n, d), gate1, bsz, t)

    ht = _modnorm(x1.reshape(bsz, t, d), norm2_g, scale2, shift2, transpose=True)
    qt = _matmul(w_pq.T.astype(BF16), ht, F32, "peer_q")
    st, stats = _route(qt, peer_keys)
    y = _peer(ht, peer_u.astype(BF16), peer_v.astype(BF16), st, stats)
    return _combine(x1.reshape(bsz, t, d), y.reshape(bsz, t, d), gate2)


def kernel(x, c, w_ada, b_ada, norm1_g, norm2_g, w_in, w_out, w_pool, pool_scale, q_norm_g, k_norm_g, cmp_pe,
           cmp_w1, cmp_w2, w_pq, peer_keys, peer_u, peer_v):
    for l in range(w_ada.shape[0]):
        x = _layer(x, c, w_ada[l], b_ada[l], norm1_g[l], norm2_g[l], w_in[l], w_out[l], w_pool[l], pool_scale[l],
                   q_norm_g[l], k_norm_g[l], cmp_pe[l], cmp_w1[l], cmp_w2[l], w_pq[l], peer_keys[l], peer_u[l],
                   peer_v[l])
    return x
```

```python
import functools
import math

import numpy as np
import jax
import jax.numpy as jnp
from jax import lax
from jax.experimental import pallas as pl
from jax.experimental.pallas import tpu as pltpu

F32 = jnp.float32
BF16 = jnp.bfloat16

HEAD_DIM = 128
POOL_WINDOWS = (2, 4, 8, 16)
NSA_GROUP = 6
NSA_KV_HEADS = 4
N_BRANCH = 3
CMP_BLOCK = 32
CMP_STRIDE = 16
SEL_BLOCK = 64
SEL_TOPN = 16
WINDOW = 512
FORCE_BONUS = 1000.0
NEG_SCORE = -1e9
ROPE_THETA = 10000.0
PEER_HEADS = 8
PEER_NKEYS = 128
PEER_TOPK = 16
NORM_EPS = 1e-6
N_MOD = 6

V7X_VMEM_BYTES = 64 * 1024 * 1024
VMEM_LIMIT = V7X_VMEM_BYTES - 6 * 1024 * 1024
LANES = 128
MASKED = -1e30


def _params(*sem):
    return pltpu.CompilerParams(dimension_semantics=sem, vmem_limit_bytes=VMEM_LIMIT)


def _pick(n, pref):
    t = min(n, pref)
    while n % t:
        t //= 2
    return t


def _gelu(x):
    return 0.5 * x * (1.0 + lax.erf(x * (1.0 / math.sqrt(2.0))))


def _ada_kernel(c_ref, w_ref, b_ref, o_ref):
    c = c_ref[...]
    a = (c * jax.nn.sigmoid(c)).astype(BF16)
    o_ref[...] = jnp.dot(a, w_ref[...].astype(BF16), preferred_element_type=F32) + b_ref[...]


def _ada(c, w_ada, b_ada):
    bsz, d = c.shape
    n = w_ada.shape[1]
    tn = _pick(n, 512)
    return pl.pallas_call(
        _ada_kernel,
        grid=(n // tn,),
        in_specs=[pl.BlockSpec((bsz, d), lambda j: (0, 0)),
                  pl.BlockSpec((d, tn), lambda j: (0, j)),
                  pl.BlockSpec((1, tn), lambda j: (0, j))],
        out_specs=pl.BlockSpec((bsz, tn), lambda j: (0, j)),
        out_shape=jax.ShapeDtypeStruct((bsz, n), F32),
        compiler_params=_params("parallel"),
        name="ada",
    )(c, w_ada, b_ada.reshape(1, n))


def _modnorm_kernel(x_ref, g_ref, sc_ref, sh_ref, o_ref, *, transpose):
    x = x_ref[0]
    ms = jnp.mean(x * x, axis=-1, keepdims=True)
    y = x * lax.rsqrt(ms + NORM_EPS) * g_ref[...]
    y = y * (1.0 + sc_ref[0]) + sh_ref[0]
    if transpose:
        o_ref[...] = y.T.astype(BF16)
    else:
        o_ref[0] = y.astype(BF16)


def _modnorm(x, g, scale, shift, transpose):
    bsz, t, d = x.shape
    tt = _pick(t, 256)
    nt = t // tt
    if transpose:
        out_spec = pl.BlockSpec((d, tt), lambda b, i: (0, b * nt + i))
        out_shape = jax.ShapeDtypeStruct((d, bsz * t), BF16)
    else:
        out_spec = pl.BlockSpec((1, tt, d), lambda b, i: (b, i, 0))
        out_shape = jax.ShapeDtypeStruct((bsz, t, d), BF16)
    return pl.pallas_call(
        functools.partial(_modnorm_kernel, transpose=transpose),
        grid=(bsz, nt),
        in_specs=[pl.BlockSpec((1, tt, d), lambda b, i: (b, i, 0)),
                  pl.BlockSpec((1, d), lambda b, i: (0, 0)),
                  pl.BlockSpec((1, 1, d), lambda b, i: (b, 0, 0)),
                  pl.BlockSpec((1, 1, d), lambda b, i: (b, 0, 0))],
        out_specs=out_spec,
        out_shape=out_shape,
        compiler_params=_params("parallel", "parallel"),
        name="modnorm_t" if transpose else "modnorm",
    )(x, g.reshape(1, d), scale.reshape(bsz, 1, d), shift.reshape(bsz, 1, d))


def _mm_kernel(a_ref, w_ref, o_ref):
    o_ref[...] = jnp.dot(a_ref[...], w_ref[...], preferred_element_type=F32).astype(o_ref.dtype)


def _matmul(a, w, out_dtype, name):
    m, k = a.shape
    n = w.shape[1]
    tm = _pick(m, 1024)
    tn = _pick(n, 1024)
    return pl.pallas_call(
        _mm_kernel,
        grid=(m // tm, n // tn),
        in_specs=[pl.BlockSpec((tm, k), lambda i, j: (i, 0)),
                  pl.BlockSpec((k, tn), lambda i, j: (0, j))],
        out_specs=pl.BlockSpec((tm, tn), lambda i, j: (i, j)),
        out_shape=jax.ShapeDtypeStruct((m, n), out_dtype),
        compiler_params=_params("parallel", "parallel"),
        name=name,
    )(a, w)


def _pool_kernel(p_ref, w_ref, s_ref, o_ref):
    t = p_ref.shape[0]
    gw = w_ref.shape[1]
    row = lax.broadcasted_iota(jnp.int32, (t, 1), 0)

    def shifted(v, k):
        return jnp.where(row >= k, pltpu.roll(v, k, axis=0), 0.0)

    for gi, win in enumerate(POOL_WINDOWS):
        p = p_ref[:, gi * gw:(gi + 1) * gw].astype(F32)
        acc = p
        span = 1
        while span < win:
            acc = acc + shifted(acc, span)
            span *= 2
        count = jnp.minimum(row + 1, win).astype(F32)
        dlt = acc / count - p
        y = jnp.dot(dlt.astype(BF16), w_ref[gi], preferred_element_type=F32)
        o_ref[:, gi * gw:(gi + 1) * gw] = (y * s_ref[:, gi * gw:(gi + 1) * gw]).astype(o_ref.dtype)


def _pool(proj, w_pool, pool_scale, bsz, t):
    ng, gw, _ = w_pool.shape
    pw = ng * gw
    return pl.pallas_call(
        _pool_kernel,
        grid=(bsz,),
        in_specs=[pl.BlockSpec((t, pw), lambda b: (b, 0)),
                  pl.BlockSpec((ng, gw, gw), lambda b: (0, 0, 0)),
                  pl.BlockSpec((1, pw), lambda b: (0, 0))],
        out_specs=pl.BlockSpec((t, pw), lambda b: (b, 0)),
        out_shape=jax.ShapeDtypeStruct((bsz * t, pw), BF16),
        compiler_params=_params("parallel"),
        name="pool",
    )(proj, w_pool.astype(BF16), pool_scale.reshape(1, pw))


def _headnorm_kernel(x_ref, g_ref, cos_ref, sin_ref, *o_refs, n_heads, out_scale, with_plain):
    cosf = cos_ref[...]
    sinf = sin_ref[...]
    g = g_ref[0]
    for h in range(n_heads):
        sl = slice(h * HEAD_DIM, (h + 1) * HEAD_DIM)
        x = x_ref[:, sl].astype(F32)
        ms = jnp.mean(x * x, axis=-1, keepdims=True)
        y = x * lax.rsqrt(ms + NORM_EPS) * g
        rot = y * cosf + pltpu.roll(y, HEAD_DIM // 2, axis=1) * sinf
        if with_plain:
            o_refs[0][:, sl] = (y * out_scale).astype(BF16)
            o_refs[1][:, sl] = (rot * out_scale).astype(BF16)
        else:
            o_refs[0][:, sl] = (rot * out_scale).astype(BF16)


def _rope_tables(t):
    half = HEAD_DIM // 2
    inv = ROPE_THETA ** (-np.arange(half, dtype=np.float64) / half)
    ang = np.arange(t, dtype=np.float64)[:, None] * inv[None, :]
    cosf = np.concatenate([np.cos(ang), np.cos(ang)], axis=-1)
    sinf = np.concatenate([-np.sin(ang), np.sin(ang)], axis=-1)
    return jnp.asarray(cosf, F32), jnp.asarray(sinf, F32)


def _headnorm(proj, gamma, t, width, col_of, n_col_blocks, gamma_of, out_scale, with_plain, name):
    n = proj.shape[0]
    tt = _pick(t, 512)
    nt = t // tt
    cosf, sinf = _rope_tables(t)
    n_out = 2 if with_plain else 1
    out_w = n_col_blocks * width
    out = pl.pallas_call(
        functools.partial(_headnorm_kernel, n_heads=width // HEAD_DIM, out_scale=out_scale, with_plain=with_plain),
        grid=(n // tt, n_col_blocks),
        in_specs=[pl.BlockSpec((tt, width), lambda i, j: (i, col_of(j))),
                  pl.BlockSpec((1, 1, HEAD_DIM), lambda i, j: (gamma_of(j), 0, 0)),
                  pl.BlockSpec((tt, HEAD_DIM), lambda i, j: (i % nt, 0)),
                  pl.BlockSpec((tt, HEAD_DIM), lambda i, j: (i % nt, 0))],
        out_specs=[pl.BlockSpec((tt, width), lambda i, j: (i, j))] * n_out,
        out_shape=[jax.ShapeDtypeStruct((n, out_w), BF16)] * n_out,
        compiler_params=_params("parallel", "parallel"),
        name=name,
    )(proj, gamma, cosf, sinf)
    return out if with_plain else out[0]


def _compress_kernel(x_ref, pe_ref, w1_ref, w2_ref, g_ref, o_ref, xs_ref):
    t = x_ref.shape[0]
    nblk = t // CMP_STRIDE
    kv = pl.program_id(1)
    xs_ref[...] = x_ref[...].astype(F32)
    cols = [xs_ref[pl.ds(l, nblk, stride=CMP_STRIDE), :] for l in range(CMP_STRIDE)]
    xc = jnp.concatenate(cols, axis=1)
    lo = jnp.dot((xc + pe_ref[0, 0]).astype(BF16), w1_ref[0, 0], preferred_element_type=F32)
    hi = jnp.dot((xc + pe_ref[0, 1]).astype(BF16), w1_ref[0, 1], preferred_element_type=F32)
    pre = lo + pltpu.roll(hi, nblk - 1, axis=0)
    hid = _gelu(pre)
    out = jnp.dot(hid.astype(BF16), w2_ref[0], preferred_element_type=F32)
    ms = jnp.mean(out * out, axis=-1, keepdims=True)
    normed = out * lax.rsqrt(ms + NORM_EPS) * g_ref[...]
    out = jnp.where(kv == 0, normed, out)
    row = lax.broadcasted_iota(jnp.int32, out.shape, 0)
    o_ref[0, 0, 0] = jnp.where(row < nblk - 1, out, 0.0).astype(BF16)


def _compress(proj, cmp_pe, cmp_w1, cmp_w2, k_gamma, bsz, t, col0):
    g = NSA_KV_HEADS
    nblk = t // CMP_STRIDE
    hidden = cmp_w1.shape[-1]
    halves = CMP_BLOCK // CMP_STRIDE
    w1 = cmp_w1.astype(BF16).reshape(2, halves, CMP_STRIDE * HEAD_DIM, hidden)
    pe = cmp_pe.reshape(2, halves, 1, CMP_STRIDE * HEAD_DIM)
    cb0 = col0 // HEAD_DIM
    return pl.pallas_call(
        _compress_kernel,
        grid=(bsz, 2, g),
        in_specs=[pl.BlockSpec((t, HEAD_DIM), lambda b, kv, gi: (b, cb0 + kv * g + gi)),
                  pl.BlockSpec((1, halves, 1, CMP_STRIDE * HEAD_DIM), lambda b, kv, gi: (kv, 0, 0, 0)),
                  pl.BlockSpec((1, halves, CMP_STRIDE * HEAD_DIM, hidden), lambda b, kv, gi: (kv, 0, 0, 0)),
                  pl.BlockSpec((1, hidden, HEAD_DIM), lambda b, kv, gi: (kv, 0, 0)),
                  pl.BlockSpec((1, HEAD_DIM), lambda b, kv, gi: (0, 0))],
        out_specs=pl.BlockSpec((1, 1, 1, nblk, HEAD_DIM), lambda b, kv, gi: (kv, b, gi, 0, 0)),
        out_shape=jax.ShapeDtypeStruct((2, bsz, g, nblk, HEAD_DIM), BF16),
        scratch_shapes=[pltpu.VMEM((t, HEAD_DIM), F32)],
        compiler_params=_params("parallel", "parallel", "parallel"),
        name="compress",
    )(proj, pe, w1, cmp_w2.astype(BF16), k_gamma.reshape(1, HEAD_DIM))


def _split3(x):
    a = x.astype(BF16)
    r = x - a.astype(F32)
    b = r.astype(BF16)
    c = (r - b.astype(F32)).astype(BF16)
    return a, b, c


def _cmp_attn_kernel(q_ref, kc_ref, vc_ref, map_ref, oc_ref, sel_ref, *, n_sel, top_n):
    tq = q_ref.shape[0]
    ncmp = kc_ref.shape[3]
    r = NSA_GROUP
    t0 = pl.program_id(1) * tq
    tcol = t0 + lax.broadcasted_iota(jnp.int32, (tq, 1), 0)
    tcol_r = jnp.concatenate([tcol] * r, axis=0)
    nrow = lax.broadcasted_iota(jnp.int32, (1, ncmp), 1)
    cmp_ok = (nrow * CMP_STRIDE + (CMP_BLOCK - 1) <= tcol_r) & (nrow < ncmp - 1)
    jrow = lax.broadcasted_iota(jnp.int32, (1, ncmp), 1)
    cur = tcol // SEL_BLOCK
    forced = (jrow == 0) | (jrow == cur) | (jrow == cur - 1)
    valid = jrow * SEL_BLOCK <= tcol
    jsub = lax.broadcasted_iota(jnp.int32, (n_sel, tq), 0)
    sel_rows = []
    for g in range(NSA_KV_HEADS):
        q = jnp.concatenate(
            [q_ref[:, (g * r + i) * HEAD_DIM:(g * r + i + 1) * HEAD_DIM] for i in range(r)], axis=0)
        s = lax.dot_general(q, kc_ref[0, 0, g], (((1,), (1,)), ((), ())), preferred_element_type=F32)
        s = jnp.where(cmp_ok, s, MASKED)
        m = jnp.max(s, axis=-1, keepdims=True)
        p = jnp.where(cmp_ok, jnp.exp(s - m), 0.0)
        den = jnp.sum(p, axis=-1, keepdims=True)
        p = p / jnp.where(den > 0, den, 1.0)
        o = jnp.dot(p.astype(BF16), vc_ref[0, 0, g], preferred_element_type=F32)
        for i in range(r):
            oc_ref[:, (g * r + i) * HEAD_DIM:(g * r + i + 1) * HEAD_DIM] = o[i * tq:(i + 1) * tq].astype(oc_ref.dtype)
        psum = p[0:tq]
        for i in range(1, r):
            psum = psum + p[i * tq:(i + 1) * tq]
        imp = sum(jnp.dot(piece, map_ref[...], preferred_element_type=F32) for piece in _split3(psum))
        score = jnp.where(valid, imp + jnp.where(forced, FORCE_BONUS, 0.0), NEG_SCORE)
        st = score.T[0:n_sel]
        rank = jnp.zeros((n_sel, tq), F32)
        for i in range(n_sel):
            ri = st[i:i + 1]
            beats = jnp.where(ri > st, 1.0, jnp.where((ri == st) & (jsub > i), 1.0, 0.0))
            rank = rank + beats
        sel_rows.append(jnp.where(rank < top_n, 1.0, 0.0))
    pad = LANES - NSA_KV_HEADS * n_sel
    if pad:
        sel_rows.append(jnp.zeros((pad, tq), F32))
    sel_ref[...] = jnp.concatenate(sel_rows, axis=0).T.astype(sel_ref.dtype)


def _sel_map(ncmp_pad, n_sel):
    cst = np.arange(ncmp_pad)[:, None] * CMP_STRIDE
    sst = np.arange(LANES)[None, :] * SEL_BLOCK
    ov = np.clip(np.minimum(cst + CMP_BLOCK, sst + SEL_BLOCK) - np.maximum(cst, sst), 0, None) / CMP_BLOCK
    ov[:, n_sel:] = 0.0
    ov[ncmp_pad - 1:, :] = 0.0
    return jnp.asarray(ov, BF16)


def _cmp_attn(qn, kvc, bsz, t):
    n, qw = qn.shape
    ncmp = kvc.shape[3]
    n_sel = t // SEL_BLOCK
    top_n = min(SEL_TOPN, n_sel)
    assert NSA_KV_HEADS * n_sel <= LANES and ncmp <= LANES
    tq = _pick(t, 256)
    nq = t // tq
    kv_spec = lambda kv: pl.BlockSpec((1, 1, NSA_KV_HEADS, ncmp, HEAD_DIM), lambda b, i: (kv, b, 0, 0, 0))
    return pl.pallas_call(
        functools.partial(_cmp_attn_kernel, n_sel=n_sel, top_n=top_n),
        grid=(bsz, nq),
        in_specs=[pl.BlockSpec((tq, qw), lambda b, i: (b * nq + i, 0)),
                  kv_spec(0), kv_spec(1),
                  pl.BlockSpec((ncmp, LANES), lambda b, i: (0, 0))],
        out_specs=[pl.BlockSpec((tq, qw), lambda b, i: (b * nq + i, 0)),
                   pl.BlockSpec((tq, LANES), lambda b, i: (b * nq + i, 0))],
        out_shape=[jax.ShapeDtypeStruct((n, qw), BF16), jax.ShapeDtypeStruct((n, LANES), BF16)],
        compiler_params=_params("parallel", "parallel"),
        name="cmp_attn",
    )(qn, kvc, kvc, _sel_map(ncmp, n_sel))


def _two_pass_attention(q_ref, k_ref, v_ref, lo, hi, bias_fn, s_ref, m_ref, acc_ref, tq, tk):
    rows = q_ref.shape[0]
    heads = rows // tq
    half = tk // 2
    m_ref[...] = jnp.full(m_ref.shape, MASKED, F32)

    def scores(j, carry):
        k = k_ref[pl.ds(pl.multiple_of(j * tk, tk), tk), :]
        s = lax.dot_general(q_ref[...], k, (((1,), (1,)), ((), ())), preferred_element_type=F32)
        bias = bias_fn(j)
        for h in range(heads):
            sl = slice(h * tq, (h + 1) * tq)
            sh = s[sl] + bias
            s_ref[j, sl, :] = sh
            m_ref[sl, :] = jnp.maximum(m_ref[sl, :], jnp.maximum(sh[:, :half], sh[:, half:]))
        return carry

    lax.fori_loop(lo, hi, scores, 0)
    m_ref[...] = jnp.broadcast_to(jnp.max(m_ref[...], axis=-1, keepdims=True), m_ref.shape)
    acc_ref[...] = jnp.zeros(acc_ref.shape, F32)
    ones = jnp.ones((tk, HEAD_DIM), BF16)

    def values(j, carry):
        m = m_ref[...]
        s = s_ref[j]
        p = jnp.concatenate([jnp.exp(s[:, :half] - m).astype(BF16), jnp.exp(s[:, half:] - m).astype(BF16)], axis=1)
        v = v_ref[pl.ds(pl.multiple_of(j * tk, tk), tk), :]
        acc_ref[...] += jnp.dot(p, jnp.concatenate([v, ones], axis=1), preferred_element_type=F32)
        return carry

    lax.fori_loop(lo, hi, values, 0)
    return acc_ref[:, :HEAD_DIM] / acc_ref[:, HEAD_DIM:]


def _sw_attn_kernel(q_ref, ks_ref, vs_ref, kw_ref, vw_ref, sel_ref, oc_ref, gate_ref, o_ref,
                    q6_ref, s_ref, m_ref, acc_ref, *, n_sel):
    tq = q_ref.shape[0]
    tk = s_ref.shape[2]
    r = NSA_GROUP
    g = pl.program_id(1)
    t0 = pl.program_id(2) * tq
    for h in range(r):
        q6_ref[h * tq:(h + 1) * tq, :] = q_ref[:, h * HEAD_DIM:(h + 1) * HEAD_DIM]
    tpos = t0 + lax.broadcasted_iota(jnp.int32, (tq, 1), 0)
    lane = lax.broadcasted_iota(jnp.int32, (1, tk), 1)
    crow = lax.broadcasted_iota(jnp.int32, (LANES, 1), 0)

    def sel_bias(j):
        kpos = j * tk + lane
        expand = jnp.where(crow == g * n_sel + kpos // SEL_BLOCK, 1.0, 0.0).astype(BF16)
        chosen = jnp.dot(sel_ref[...], expand, preferred_element_type=F32)
        return jnp.where(kpos <= tpos, jnp.where(chosen > 0.5, 0.0, MASKED), MASKED)

    def win_bias(j):
        dist = tpos - (j * tk + lane)
        return jnp.where(dist >= 0, jnp.where(dist < WINDOW, 0.0, MASKED), MASKED)

    hi = (t0 + tq + tk - 1) // tk
    scratch = (s_ref, m_ref, acc_ref, tq, tk)
    o_s = _two_pass_attention(q6_ref, ks_ref, vs_ref, 0, hi, sel_bias, *scratch)
    lo_w = jnp.maximum(t0 - (WINDOW - 1), 0) // tk
    o_w = _two_pass_attention(q6_ref, kw_ref, vw_ref, lo_w, hi, win_bias, *scratch)
    gate = jax.nn.sigmoid(gate_ref[...]).astype(BF16)
    src = lax.broadcasted_iota(jnp.int32, (LANES, r * HEAD_DIM), 0)
    dst = lax.broadcasted_iota(jnp.int32, (LANES, r * HEAD_DIM), 1) // HEAD_DIM
    wide = [jnp.dot(gate, jnp.where(src == N_BRANCH * dst + br, 1.0, 0.0).astype(BF16), preferred_element_type=F32)
            for br in range(N_BRANCH)]
    for h in range(r):
        rows = slice(h * tq, (h + 1) * tq)
        cols = slice(h * HEAD_DIM, (h + 1) * HEAD_DIM)
        o = wide[0][:, cols] * oc_ref[:, cols].astype(F32) + wide[1][:, cols] * o_s[rows] + wide[2][:, cols] * o_w[rows]
        o_ref[:, cols] = o.astype(o_ref.dtype)


def _sw_attn(qr, kr, proj, sel, o_c, gates, bsz, t, vsel_col, vwin_col):
    n, qw = qr.shape
    g = NSA_KV_HEADS
    gw = NSA_GROUP * HEAD_DIM
    tq = _pick(t, 256)
    tk = tq
    rows = NSA_GROUP * tq
    nq = t // tq
    n_sel = t // SEL_BLOCK
    vs0, vw0 = vsel_col // HEAD_DIM, vwin_col // HEAD_DIM
    qspec = pl.BlockSpec((tq, gw), lambda b, gi, i: (b * nq + i, gi))
    return pl.pallas_call(
        functools.partial(_sw_attn_kernel, n_sel=n_sel),
        grid=(bsz, g, nq),
        in_specs=[qspec,
                  pl.BlockSpec((t, HEAD_DIM), lambda b, gi, i: (b, gi)),
                  pl.BlockSpec((t, HEAD_DIM), lambda b, gi, i: (b, vs0 + gi)),
                  pl.BlockSpec((t, HEAD_DIM), lambda b, gi, i: (b, g + gi)),
                  pl.BlockSpec((t, HEAD_DIM), lambda b, gi, i: (b, vw0 + gi)),
                  pl.BlockSpec((tq, LANES), lambda b, gi, i: (b * nq + i, 0)),
                  qspec,
                  pl.BlockSpec((tq, LANES), lambda b, gi, i: (b * nq + i, gi))],
        out_specs=qspec,
        out_shape=jax.ShapeDtypeStruct((n, qw), BF16),
        scratch_shapes=[pltpu.VMEM((rows, HEAD_DIM), BF16), pltpu.VMEM((t // tk, rows, tk), F32),
                        pltpu.VMEM((rows, tk // 2), F32), pltpu.VMEM((rows, 2 * HEAD_DIM), F32)],
        compiler_params=_params("parallel", "parallel", "parallel"),
        name="sw_attn",
    )(qr, kr, proj, kr, proj, sel, o_c, gates)


def _outproj_kernel(a1_ref, a2_ref, w1_ref, w2_ref, x_ref, g_ref, o_ref):
    y = jnp.dot(a1_ref[...], w1_ref[...], preferred_element_type=F32)
    y = y + jnp.dot(a2_ref[...], w2_ref[...], preferred_element_type=F32)
    o_ref[...] = x_ref[...] + g_ref[0] * y


def _outproj(y_pool, y_nsa, w_out, x2d, gate, bsz, t):
    n, d = x2d.shape
    k1, k2 = y_pool.shape[1], y_nsa.shape[1]
    tm = _pick(t, 1024)
    tn = _pick(d, 1024)
    nt = t // tm
    w = w_out.astype(BF16)
    return pl.pallas_call(
        _outproj_kernel,
        grid=(n // tm, d // tn),
        in_specs=[pl.BlockSpec((tm, k1), lambda i, j: (i, 0)),
                  pl.BlockSpec((tm, k2), lambda i, j: (i, 0)),
                  pl.BlockSpec((k1, tn), lambda i, j: (0, j)),
                  pl.BlockSpec((k2, tn), lambda i, j: (0, j)),
                  pl.BlockSpec((tm, tn), lambda i, j: (i, j)),
                  pl.BlockSpec((1, 1, tn), lambda i, j: (i // nt, 0, j))],
        out_specs=pl.BlockSpec((tm, tn), lambda i, j: (i, j)),
        out_shape=jax.ShapeDtypeStruct((n, d), F32),
        compiler_params=_params("parallel", "parallel"),
        name="outproj",
    )(y_pool, y_nsa, w[:k1], w[k1:], x2d, gate.reshape(bsz, 1, d))


def _cmpx(v, i, j):
    hi = jnp.maximum(v[i], v[j])
    lo = jnp.minimum(v[i], v[j])
    v[i], v[j] = hi, lo


def _bitonic_merge_desc(v):
    n = len(v)
    d = n // 2
    while d >= 1:
        for i in range(n):
            if (i // d) % 2 == 0:
                _cmpx(v, i, i + d)
        d //= 2


def _sort_desc(v):
    n = len(v)
    size = 2
    while size <= n:
        for base in range(0, n, size):
            blk = v[base:base + size]
            blk[size // 2:] = blk[size // 2:][::-1]
            _bitonic_merge_desc(blk)
            v[base:base + size] = blk
        size *= 2


def _merge_top(a, b):
    n = len(a)
    v = [jnp.maximum(a[i], b[n - 1 - i]) for i in range(n)]
    _bitonic_merge_desc(v)
    return v


def _top_sublanes(s, k):
    nk = s.shape[0]
    groups = nk // 8
    assert groups == k
    v = [s[8 * i:8 * i + 8] for i in range(groups)]
    _sort_desc(v)
    shift = 4
    while shift >= 1:
        v = _merge_top(v, [pltpu.roll(x, shift, axis=0) for x in v])
        shift //= 2
    return v


def _route_kernel(qt_ref, keys_ref, st_ref, stat_ref):
    k = PEER_TOPK
    nh = keys_ref.shape[0]
    kd = keys_ref.shape[3]
    neg = jnp.full((8, qt_ref.shape[1]), -jnp.inf, F32)
    thr_rows, m0_rows, m1_rows, iz_rows = [], [], [], []
    for h in range(nh):
        tops = []
        for c in range(2):
            row0 = (h * 2 + c) * kd
            qa, qb, qc = _split3(qt_ref[row0:row0 + kd, :])
            ka, kb, _ = _split3(keys_ref[h, c])
            s = (jnp.dot(ka, qa, preferred_element_type=F32) + jnp.dot(ka, qb, preferred_element_type=F32)
                 + jnp.dot(kb, qa, preferred_element_type=F32) + jnp.dot(ka, qc, preferred_element_type=F32)
                 + jnp.dot(kb, qb, preferred_element_type=F32))
            st_ref[(h * 2 + c) * PEER_NKEYS:(h * 2 + c + 1) * PEER_NKEYS, :] = s
            tops.append(_top_sublanes(s, k))
        v0, v1 = tops
        cands = [v0[a] + v1[b] for a in range(k) for b in range(k) if (a + 1) * (b + 1) <= k]
        lists = []
        for base in range(0, len(cands), k):
            blk = cands[base:base + k]
            blk = blk + [neg] * (k - len(blk))
            _sort_desc(blk)
            lists.append(blk)
        best = lists[0]
        for other in lists[1:]:
            best = _merge_top(best, other)
        z = sum(jnp.exp(b - best[0]) for b in best)
        thr_rows.append(best[k - 1][0:1])
        m0_rows.append(v0[0][0:1])
        m1_rows.append(v1[0][0:1])
        iz_rows.append((1.0 / z)[0:1])
    stat_ref[...] = jnp.concatenate(thr_rows + m0_rows + m1_rows + iz_rows, axis=0)


def _route(qt, peer_keys):
    rows, n = qt.shape
    nh, _, nk, kd = peer_keys.shape
    assert nk == PEER_NKEYS and nk == 8 * PEER_TOPK
    tr = _pick(n, 512)
    return pl.pallas_call(
        _route_kernel,
        grid=(n // tr,),
        in_specs=[pl.BlockSpec((rows, tr), lambda i: (0, i)),
                  pl.BlockSpec((nh, 2, nk, kd), lambda i: (0, 0, 0, 0))],
        out_specs=[pl.BlockSpec((nh * 2 * nk, tr), lambda i: (0, i)),
                   pl.BlockSpec((4 * nh, tr), lambda i: (0, i))],
        out_shape=[jax.ShapeDtypeStruct((nh * 2 * nk, n), F32), jax.ShapeDtypeStruct((4 * nh, n), F32)],
        compiler_params=_params("parallel"),
        name="route",
    )(qt, peer_keys)


def _peer_kernel(ht_ref, u_ref, v_ref, st_ref, stat_ref, o_ref, e0_ref, e1_ref, wa_ref, new_ref, pre_ref,
                 pre_new_ref, rows_ref, *, n_tiles, n_pairs):
    s = pl.program_id(0)
    te = u_ref.shape[0]
    nk = PEER_NKEYS
    nh = e0_ref.shape[0]
    per_step = te // nk
    pair1 = jnp.clip(s - 1, 0, n_pairs - 1)

    @pl.when(s == 0)
    def _():
        o_ref[...] = jnp.zeros_like(o_ref)
        wa_ref[...] = jnp.zeros_like(wa_ref)
        pre_ref[...] = jnp.zeros_like(pre_ref)

    @pl.when((s == 0) | ((s >= 1) & ((s - 1) % n_tiles == 0)))
    def _():
        for h in range(nh):
            m0 = stat_ref[nh + h:nh + h + 1, :]
            m1 = stat_ref[2 * nh + h:2 * nh + h + 1, :]
            iz = stat_ref[3 * nh + h:3 * nh + h + 1, :]
            e0_ref[h] = jnp.exp(st_ref[(2 * h) * nk:(2 * h + 1) * nk, :] - m0)
            e1_ref[h] = jnp.exp(st_ref[(2 * h + 1) * nk:(2 * h + 2) * nk, :] - m1) * iz

    @pl.when((s >= 2) & ((s - 2) % n_tiles == 0))
    def _():
        o_ref[...] = jnp.zeros_like(o_ref)

    tm = ht_ref.shape[1]
    d = v_ref.shape[1]
    tile = pair1 % n_tiles
    n_split = 2
    th = tm // n_split
    dq = d // 4

    def first_matmul(half, row0):
        cols = slice(half * th, (half + 1) * th)
        pre_new_ref[:, cols] = jnp.dot(u_ref[pl.ds(row0, te), :], ht_ref[:, cols], preferred_element_type=F32)

    def second_matmul(q, row0):
        cols = slice(q * dq, (q + 1) * dq)
        o_ref[:, cols] += jnp.dot(wa_ref[pl.ds(row0, tm), :], v_ref[:, cols], preferred_element_type=F32)

    for a in range(per_step):
        i0 = tile * per_step + a
        for h in range(nh):
            k = (a * nh + h) * 2
            rows_ref[k:k + 1, :] = st_ref[pl.ds((2 * h) * nk + i0, 1), :]
            rows_ref[k + 1:k + 2, :] = e0_ref[h, pl.ds(i0, 1), :]

    def gate_piece(a, half):
        cols = slice(half * th, (half + 1) * th)
        w = None
        for h in range(nh):
            k = (a * nh + h) * 2
            s0 = rows_ref[k:k + 1, cols]
            e0 = rows_ref[k + 1:k + 2, cols]
            ssum = s0 + st_ref[(2 * h + 1) * nk:(2 * h + 2) * nk, cols]
            wh = jnp.where(ssum >= stat_ref[h:h + 1, cols], e0 * e1_ref[h, :, cols], 0.0)
            w = wh if w is None else w + wh
        prod = w * _gelu(pre_ref[a * nk:(a + 1) * nk, cols])
        new_ref[cols, a * nk:(a + 1) * nk] = prod.T.astype(BF16)
        folded = jnp.max(prod.reshape(nk // 8, 8, th), axis=0)
        return functools.reduce(jnp.maximum, [folded[:, c * LANES:(c + 1) * LANES] for c in range(th // LANES)])

    def zero_after(vregs):
        bits = pltpu.bitcast(functools.reduce(jnp.maximum, vregs), jnp.uint32)
        zero = lax.shift_right_logical(lax.shift_right_logical(bits, jnp.uint32(16)), jnp.uint32(16))
        return pl.multiple_of(zero[0, 0].astype(jnp.int32), 16)

    pieces = [(a, half) for a in range(per_step) for half in range(n_split)]
    per_chunk = -(-len(pieces) // 4)
    chunks = [functools.partial(first_matmul, 1)] + [functools.partial(second_matmul, q) for q in range(3)]
    first_matmul(0, 0)
    for ci, chunk in enumerate(chunks):
        chunk(zero_after([gate_piece(a, half) for a, half in pieces[ci * per_chunk:(ci + 1) * per_chunk]]))
    second_matmul(3, 0)
    wa_ref[...] = new_ref[...]
    pre_ref[...] = pre_new_ref[...]


def _peer(ht, u, v, st, stats):
    d, n = ht.shape
    e = u.shape[0]
    nh = stats.shape[0] // 4
    tm = _pick(n, 512)
    te = _pick(e, 512)
    n_tiles = e // te
    n_pairs = (n // tm) * n_tiles
    pair = lambda s, lag: jnp.clip(s - lag, 0, n_pairs - 1)
    return pl.pallas_call(
        functools.partial(_peer_kernel, n_tiles=n_tiles, n_pairs=n_pairs),
        grid=(n_pairs + 2,),
        in_specs=[pl.BlockSpec((d, tm), lambda s: (0, pair(s, 0) // n_tiles)),
                  pl.BlockSpec((te, d), lambda s: (pair(s, 0) % n_tiles, 0)),
                  pl.BlockSpec((te, d), lambda s: (pair(s, 2) % n_tiles, 0)),
                  pl.BlockSpec((st.shape[0], tm), lambda s: (0, pair(s, 1) // n_tiles)),
                  pl.BlockSpec((stats.shape[0], tm), lambda s: (0, pair(s, 1) // n_tiles))],
        out_specs=pl.BlockSpec((tm, d), lambda s: (pair(s, 2) // n_tiles, 0)),
        out_shape=jax.ShapeDtypeStruct((n, d), F32),
        scratch_shapes=[pltpu.VMEM((nh, PEER_NKEYS, tm), F32), pltpu.VMEM((nh, PEER_NKEYS, tm), F32),
                        pltpu.VMEM((tm, te), BF16), pltpu.VMEM((tm, te), BF16),
                        pltpu.VMEM((te, tm), F32), pltpu.VMEM((te, tm), F32),
                        pltpu.VMEM((2 * nh * (te // PEER_NKEYS), tm), F32)],
        compiler_params=_params("arbitrary"),
        name="peer",
    )(ht, u, v, st, stats)


def _combine_kernel(x_ref, y_ref, g_ref, o_ref):
    o_ref[0] = x_ref[0] + g_ref[0] * y_ref[0]


def _combine(x, y, gate):
    bsz, t, d = x.shape
    tt = _pick(t, 256)
    spec = pl.BlockSpec((1, tt, d), lambda b, i: (b, i, 0))
    return pl.pallas_call(
        _combine_kernel,
        grid=(bsz, t // tt),
        in_specs=[spec, spec, pl.BlockSpec((1, 1, d), lambda b, i: (b, 0, 0))],
        out_specs=spec,
        out_shape=jax.ShapeDtypeStruct((bsz, t, d), F32),
        compiler_params=_params("parallel", "parallel"),
        name="combine",
    )(x, y, gate.reshape(bsz, 1, d))


def _layer(x, c, w_ada, b_ada, norm1_g, norm2_g, w_in, w_out, w_pool, pool_scale, q_norm_g, k_norm_g,
           cmp_pe, cmp_w1, cmp_w2, w_pq, peer_keys, peer_u, peer_v):
    bsz, t, d = x.shape
    n = bsz * t
    g, r = NSA_KV_HEADS, NSA_GROUP
    pool_w = w_pool.shape[0] * w_pool.shape[1]
    nsa_w = g * r * HEAD_DIM
    kv_w = N_BRANCH * 2 * g * HEAD_DIM
    gate_w = g * r * N_BRANCH
    assert w_in.shape[1] == pool_w + nsa_w + kv_w + gate_w
    q_col, kv_col = pool_w, pool_w + nsa_w
    branch_w = g * HEAD_DIM
    assert pool_w % nsa_w in (0, pool_w) and q_col % 1024 == 0

    mod = _ada(c, w_ada, b_ada)
    shift1, scale1, gate1, shift2, scale2, gate2 = [mod[:, i * d:(i + 1) * d] for i in range(N_MOD)]

    h = _modnorm(x, norm1_g, scale1, shift1, transpose=False).reshape(n, d)
    main_w = pool_w + nsa_w + kv_w
    w_main = w_in[:, :main_w].astype(BF16)
    w_gate = w_in[:, main_w:].reshape(d, g, r * N_BRANCH)
    w_gate = jnp.pad(w_gate, ((0, 0), (0, 0), (0, LANES - r * N_BRANCH))).reshape(d, g * LANES).astype(BF16)
    proj = _matmul(h, w_main, BF16, "inproj")
    gates = _matmul(h, w_gate, F32, "gateproj")

    y_pool = _pool(proj, w_pool, pool_scale, bsz, t)

    scale = HEAD_DIM ** -0.5
    qblk = 1024
    qn, qr = _headnorm(proj, q_norm_g.reshape(1, 1, HEAD_DIM), t, qblk, lambda j: q_col // qblk + j, nsa_w // qblk,
                       lambda j: 0, scale, True, "qprep")
    ksel_blk = (kv_col + 2 * branch_w) // branch_w
    kr = _headnorm(proj, k_norm_g.reshape(N_BRANCH, 1, HEAD_DIM), t, branch_w, lambda j: ksel_blk + 2 * j, 2,
                   lambda j: 1 + j, 1.0, False, "kprep")
    kvc = _compress(proj, cmp_pe, cmp_w1, cmp_w2, k_norm_g[0], bsz, t, kv_col)
    o_c, sel = _cmp_attn(qn, kvc, bsz, t)
    y_nsa = _sw_attn(qr, kr, proj, sel, o_c, gates, bsz, t, kv_col + 3 * branch_w, kv_col + 5 * branch_w)

    x1 = _outproj(y_pool, y_nsa, w_out, x.reshape(n, d), gate1, bsz, t)

    ht = _modnorm(x1.reshape(bsz, t, d), norm2_g, scale2, shift2, transpose=True)
    qt = _matmul(w_pq.T.astype(BF16), ht, F32, "peer_q")
    st, stats = _route(qt, peer_keys)
    y = _peer(ht, peer_u.astype(BF16), peer_v.astype(BF16), st, stats)
    return _combine(x1.reshape(bsz, t, d), y.reshape(bsz, t, d), gate2)


def kernel(x, c, w_ada, b_ada, norm1_g, norm2_g, w_in, w_out, w_pool, pool_scale, q_norm_g, k_norm_g, cmp_pe,
           cmp_w1, cmp_w2, w_pq, peer_keys, peer_u, peer_v):
    for l in range(w_ada.shape[0]):
        x = _layer(x, c, w_ada[l], b_ada[l], norm1_g[l], norm2_g[l], w_in[l], w_out[l], w_pool[l], pool_scale[l],
                   q_norm_g[l], k_norm_g[l], cmp_pe[l], cmp_w1[l], cmp_w2[l], w_pq[l], peer_keys[l], peer_u[l],
                   peer_v[l])
    return x
```
